```python
import jax
import jax.numpy as jnp
from jax import lax
import numpy as np

D_MODEL = 1024
BATCH = 8
SEQ = 4096
DEPTH = 2
DEC_BATCH = 128
DEC_SEQ = 1
PAST_LEN = 16384
PAGE_SIZE = 128

N_BRANCH = 4
BRANCH_WIDTH = D_MODEL // 2
SC_CONV_W = 3
CF_CONV_W = 31
N_HEADS = 8
N_KV_HEADS = 2
HEAD_DIM = 64
GROUP = N_HEADS // N_KV_HEADS
WINDOW = 128
ATTN_BLOCK = 128
ROPE_THETA = 10000.0
GLA_HEADS = 4
GLA_DK = 64
GLA_DV = BRANCH_WIDTH // GLA_HEADS
GLA_GATE_RANK = 16
GLA_GATE_NORM = 16.0
GLA_CHUNK = 64
D_FF = -(-8 * D_MODEL // (3 * 256)) * 256
NEG_INF = -1e30
RMS_EPS = 1e-6
LN_EPS = 1e-5

IN_SPLITS = (BRANCH_WIDTH, BRANCH_WIDTH, BRANCH_WIDTH,
             BRANCH_WIDTH, BRANCH_WIDTH,
             N_HEADS * HEAD_DIM, N_KV_HEADS * HEAD_DIM, N_KV_HEADS * HEAD_DIM,
             GLA_HEADS * GLA_DK, GLA_HEADS * GLA_DK, GLA_HEADS * GLA_DV,
             GLA_HEADS * GLA_DV, GLA_GATE_RANK,
             N_BRANCH * D_MODEL)
IN_WIDTH = sum(IN_SPLITS)
IN_SPLIT_POINTS = tuple(int(s) for s in np.cumsum(IN_SPLITS)[:-1])

kernel_name = 'hybrid_gated_branch_decoder_step'


def rms_norm(x, g):
    xf = x.astype(jnp.float32)
    y = xf * lax.rsqrt(jnp.mean(xf * xf, axis=-1, keepdims=True) + RMS_EPS)
    return (y * g.astype(jnp.float32)).astype(x.dtype)


def layer_norm(x, g, b):
    xf = x.astype(jnp.float32)
    mu = jnp.mean(xf, axis=-1, keepdims=True)
    xc = xf - mu
    y = xc * lax.rsqrt(jnp.mean(xc * xc, axis=-1, keepdims=True) + LN_EPS)
    return (y * g.astype(jnp.float32) + b.astype(jnp.float32)).astype(x.dtype)


def rope(x, pos):
    half = HEAD_DIM // 2
    inv = ROPE_THETA ** (-jnp.arange(half, dtype=jnp.float32) / half)
    ang = pos.astype(jnp.float32)[:, None] * inv[None, :]
    cos = jnp.cos(ang)[None, :, None, :]
    sin = jnp.sin(ang)[None, :, None, :]
    xf = x.astype(jnp.float32)
    x1, x2 = xf[..., :half], xf[..., half:]
    return jnp.concatenate([x1 * cos - x2 * sin, x2 * cos + x1 * sin], axis=-1).astype(x.dtype)


def causal_dwconv(buf, u, w, b=None):
    width = w.shape[0]
    ucat = jnp.concatenate([buf.astype(u.dtype), u], axis=1)
    y = lax.conv_general_dilated(ucat, w[:, None, :].astype(u.dtype), window_strides=(1,), padding='VALID',
                                 dimension_numbers=('NWC', 'WIO', 'NWC'), feature_group_count=u.shape[-1])
    if b is not None:
        y = y + b
    return y, ucat[:, ucat.shape[1] - (width - 1):]


def sliding_window_attention(q, k, v, k_buf, v_buf, start, sinks):
    B, L = q.shape[0], q.shape[1]
    W = k_buf.shape[1]
    qb_size = min(ATTN_BLOCK, L)
    nb = -(-L // qb_size)
    Lp = nb * qb_size
    pad = ((0, 0), (0, Lp - L), (0, 0), (0, 0))
    new_k_buf = jnp.concatenate([k_buf, k], axis=1)[:, -WINDOW:]
    new_v_buf = jnp.concatenate([v_buf, v], axis=1)[:, -WINDOW:]
    k_all = jnp.concatenate([k_buf, jnp.pad(k, pad)], axis=1)
    v_all = jnp.concatenate([v_buf, jnp.pad(v, pad)], axis=1)
    idx = (jnp.arange(nb) * qb_size)[:, None] + jnp.arange(qb_size + W)[None, :]
    kb = k_all[:, idx]
    vb = v_all[:, idx]
    qb = jnp.pad(q, pad).reshape(B, nb, qb_size, N_KV_HEADS, GROUP, HEAD_DIM)
    s = jnp.einsum('bnqkgd,bnskd->bnkgqs', qb, kb, preferred_element_type=jnp.float32) * (HEAD_DIM ** -0.5)
    qpos = start + jnp.arange(Lp).reshape(nb, qb_size)
    kpos = start - W + idx
    delta = qpos[:, :, None] - kpos[:, None, :]
    valid = (delta >= 0) & (delta < WINDOW) & (kpos[:, None, :] >= 0)
    s = jnp.where(valid[None, :, None, None], s, NEG_INF)
    sink = sinks.astype(jnp.float32).reshape(1, 1, N_KV_HEADS, GROUP, 1)
    m = jnp.maximum(jnp.max(s, axis=-1), sink)
    p = jnp.exp(s - m[..., None])
    denom = jnp.sum(p, axis=-1) + jnp.exp(sink - m)
    p = (p / denom[..., None]).astype(v.dtype)
    o = jnp.einsum('bnkgqs,bnskd->bnqkgd', p, vb).reshape(B, Lp, N_HEADS * HEAD_DIM)[:, :L]
    return o, new_k_buf, new_v_buf


def gla_recurrence(q, k, v, log_a, S0):
    B, L = q.shape[0], q.shape[1]
    C = min(GLA_CHUNK, L)
    n = -(-L // C)
    Lp = n * C
    pad = ((0, 0), (0, Lp - L), (0, 0), (0, 0))
    f32 = jnp.float32
    qf = jnp.pad(q.astype(f32) * (GLA_DK ** -0.5), pad).reshape(B, n, C, GLA_HEADS, GLA_DK)
    kf = jnp.pad(k.astype(f32), pad).reshape(B, n, C, GLA_HEADS, GLA_DK)
    vf = jnp.pad(v.astype(f32), pad).reshape(B, n, C, GLA_HEADS, GLA_DV)
    la = jnp.pad(log_a.astype(f32), pad).reshape(B, n, C, GLA_HEADS, GLA_DK)
    bcum = jnp.cumsum(la, axis=2)
    b_last = bcum[:, :, -1:]
    qe = qf * jnp.exp(bcum)
    ke = kf * jnp.exp(-bcum)
    kd = kf * jnp.exp(b_last - bcum)
    causal = jnp.tril(jnp.ones((C, C), dtype=bool))
    A = jnp.where(causal, jnp.einsum('bnthd,bnshd->bnhts', qe, ke), 0.0)
    o_intra = jnp.einsum('bnhts,bnshv->bnthv', A, vf)
    decay = jnp.exp(b_last[:, :, 0])

    def step(S, inp):
        qe_c, kd_c, v_c, dec_c = inp
        o_c = jnp.einsum('bthd,bhdv->bthv', qe_c, S)
        S = dec_c[..., None] * S + jnp.einsum('bthd,bthv->bhdv', kd_c, v_c)
        return S, o_c

    xs = (jnp.moveaxis(qe, 1, 0), jnp.moveaxis(kd, 1, 0), jnp.moveaxis(vf, 1, 0), jnp.moveaxis(decay, 1, 0))
    S, o_inter = lax.scan(step, S0.astype(f32), xs)
    o = o_intra + jnp.moveaxis(o_inter, 0, 1)
    return o.reshape(B, Lp, GLA_HEADS, GLA_DV)[:, :L], S


def trunk_layer(x, start, sc_buf, cf_buf, k_buf, v_buf, gla_state,
                norm_mix, w_in, conv_short, conv_conf, conv_conf_b, conf_ln_g, conf_ln_b,
                q_norm, k_norm, attn_sinks, w_gla_gate, b_gla_gate, gla_norm, w_branch, w_out,
                norm_ffn, w_ffn_in, w_ffn_out):
    B, L, _ = x.shape
    dt = x.dtype
    xn = rms_norm(x, norm_mix)
    proj = xn @ w_in
    (sc_h, sc_b, sc_c, cf_a, cf_b, aq, ak, av, gq, gk, gv, gg, ga, gates) = jnp.split(proj, IN_SPLIT_POINTS, axis=-1)
    conv_u, new_sc = causal_dwconv(sc_buf, sc_c * sc_h, conv_short)
    o_a = sc_b * conv_u
    c, new_cf = causal_dwconv(cf_buf, cf_a * jax.nn.sigmoid(cf_b), conv_conf, conv_conf_b)
    o_b = jax.nn.silu(layer_norm(c, conf_ln_g, conf_ln_b))
    pos = start + jnp.arange(L)
    q = rope(rms_norm(aq.reshape(B, L, N_HEADS, HEAD_DIM), q_norm), pos)
    k = rope(rms_norm(ak.reshape(B, L, N_KV_HEADS, HEAD_DIM), k_norm), pos)
    v = av.reshape(B, L, N_KV_HEADS, HEAD_DIM)
    o_c, new_k, new_v = sliding_window_attention(q, k, v, k_buf.astype(dt), v_buf.astype(dt), start, attn_sinks)
    log_a = jax.nn.log_sigmoid((ga @ w_gla_gate + b_gla_gate).astype(jnp.float32)) / GLA_GATE_NORM
    o_d, new_S = gla_recurrence(gq.reshape(B, L, GLA_HEADS, GLA_DK), gk.reshape(B, L, GLA_HEADS, GLA_DK),
                                gv.reshape(B, L, GLA_HEADS, GLA_DV), log_a.reshape(B, L, GLA_HEADS, GLA_DK), gla_state)
    o_d = (rms_norm(o_d.astype(dt), gla_norm) * jax.nn.silu(gg.reshape(B, L, GLA_HEADS, GLA_DV))).reshape(B, L, BRANCH_WIDTH)
    branches = jnp.stack([o_a, o_b, o_c, o_d], axis=2)
    branch_d = jnp.einsum('blrc,rcd->blrd', branches, w_branch)
    g = jax.nn.sigmoid(gates.reshape(B, L, N_BRANCH, D_MODEL))
    merged = jnp.einsum('blrd,blrd->bld', g, branch_d)
    x = x + merged @ w_out
    hu = rms_norm(x, norm_ffn) @ w_ffn_in
    h_gate, h_up = jnp.split(hu, 2, axis=-1)
    x = x + (jax.nn.silu(h_gate) * h_up) @ w_ffn_out
    return x, new_sc, new_cf, new_k, new_v, new_S.astype(dt)


def setup_inputs(seed: int = 0) -> dict:
    key = jax.random.key(seed)
    ks = jax.random.split(key, 25)
    f32 = jnp.float32

    def nrm(k, shape, scale):
        return scale * jax.random.normal(k, shape, f32)

    def gain(k, shape):
        return 1.0 + 0.02 * jax.random.normal(k, shape, f32)

    win_buf = min(WINDOW, PAST_LEN)
    return {
        'x_prompt': nrm(ks[0], (BATCH, SEQ, D_MODEL), 1.0),
        'x_sample': nrm(ks[1], (DEC_BATCH, DEC_SEQ, D_MODEL), 1.0),
        'cache_swa_k': nrm(ks[2], (DEPTH, DEC_BATCH, win_buf, N_KV_HEADS, HEAD_DIM), 1.0),
        'cache_swa_v': nrm(ks[3], (DEPTH, DEC_BATCH, win_buf, N_KV_HEADS, HEAD_DIM), 1.0),
        'state_sconv': nrm(ks[4], (DEPTH, DEC_BATCH, SC_CONV_W - 1, BRANCH_WIDTH), 1.0),
        'state_cconv': nrm(ks[5], (DEPTH, DEC_BATCH, CF_CONV_W - 1, BRANCH_WIDTH), 0.5),
        'state_gla': nrm(ks[6], (DEPTH, DEC_BATCH, GLA_HEADS, GLA_DK, GLA_DV), 1.0),
        'norm_mix': gain(ks[7], (DEPTH, D_MODEL)),
        'w_in': nrm(ks[8], (DEPTH, D_MODEL, IN_WIDTH), D_MODEL ** -0.5),
        'conv_short': nrm(ks[9], (DEPTH, SC_CONV_W, BRANCH_WIDTH), SC_CONV_W ** -0.5),
        'conv_conf': nrm(ks[10], (DEPTH, CF_CONV_W, BRANCH_WIDTH), CF_CONV_W ** -0.5),
        'conv_conf_b': nrm(ks[11], (DEPTH, BRANCH_WIDTH), 0.02),
        'conf_ln_g': gain(ks[12], (DEPTH, BRANCH_WIDTH)),
        'conf_ln_b': nrm(ks[13], (DEPTH, BRANCH_WIDTH), 0.02),
        'q_norm': gain(ks[14], (DEPTH, HEAD_DIM)),
        'k_norm': gain(ks[15], (DEPTH, HEAD_DIM)),
        'attn_sinks': nrm(ks[16], (DEPTH, N_HEADS), 0.5),
        'w_gla_gate': nrm(ks[17], (DEPTH, GLA_GATE_RANK, GLA_HEADS * GLA_DK), GLA_GATE_RANK ** -0.5),
        'b_gla_gate': nrm(ks[18], (DEPTH, GLA_HEADS * GLA_DK), 0.1),
        'gla_norm': gain(ks[19], (DEPTH, GLA_DV)),
        'w_branch': nrm(ks[20], (DEPTH, N_BRANCH, BRANCH_WIDTH, D_MODEL), BRANCH_WIDTH ** -0.5),
        'w_out': nrm(ks[21], (DEPTH, D_MODEL, D_MODEL), D_MODEL ** -0.5),
        'norm_ffn': gain(ks[22], (DEPTH, D_MODEL)),
        'w_ffn_in': nrm(ks[23], (DEPTH, D_MODEL, 2 * D_FF), D_MODEL ** -0.5),
        'w_ffn_out': nrm(ks[24], (DEPTH, D_FF, D_MODEL), D_FF ** -0.5),
    }


def reference(x_prompt, x_sample, cache_swa_k, cache_swa_v, state_sconv, state_cconv, state_gla,
              norm_mix, w_in, conv_short, conv_conf, conv_conf_b, conf_ln_g, conf_ln_b,
              q_norm, k_norm, attn_sinks, w_gla_gate, b_gla_gate, gla_norm, w_branch, w_out,
              norm_ffn, w_ffn_in, w_ffn_out):
    dt = x_prompt.dtype
    bp = x_prompt.shape[0]
    zero_sc = jnp.zeros((bp, SC_CONV_W - 1, BRANCH_WIDTH), dt)
    zero_cf = jnp.zeros((bp, CF_CONV_W - 1, BRANCH_WIDTH), dt)
    zero_kv = jnp.zeros((bp, WINDOW, N_KV_HEADS, HEAD_DIM), dt)
    zero_S = jnp.zeros((bp, GLA_HEADS, GLA_DK, GLA_DV), dt)
    yp, ys = x_prompt, x_sample
    sc_p, cf_p, k_p, v_p, S_p = [], [], [], [], []
    sc_s, cf_s, k_s, v_s, S_s = [], [], [], [], []
    for l in range(DEPTH):
        w = (norm_mix[l], w_in[l], conv_short[l], conv_conf[l], conv_conf_b[l], conf_ln_g[l], conf_ln_b[l],
             q_norm[l], k_norm[l], attn_sinks[l], w_gla_gate[l], b_gla_gate[l], gla_norm[l], w_branch[l], w_out[l],
             norm_ffn[l], w_ffn_in[l], w_ffn_out[l])
        yp, a1, a2, a3, a4, a5 = trunk_layer(yp, 0, zero_sc, zero_cf, zero_kv, zero_kv, zero_S, *w)
        sc_p.append(a1); cf_p.append(a2); k_p.append(a3); v_p.append(a4); S_p.append(a5)
        ys, b1, b2, b3, b4, b5 = trunk_layer(ys, PAST_LEN, state_sconv[l], state_cconv[l], cache_swa_k[l],
                                             cache_swa_v[l], state_gla[l], *w)
        sc_s.append(b1); cf_s.append(b2); k_s.append(b3); v_s.append(b4); S_s.append(b5)
    return (yp, ys,
            jnp.stack(sc_p), jnp.stack(sc_s),
            jnp.stack(cf_p), jnp.stack(cf_s),
            jnp.stack(k_p), jnp.stack(k_s),
            jnp.stack(v_p), jnp.stack(v_s),
            jnp.stack(S_p), jnp.stack(S_s))
```

```python
import functools

import numpy as np
import jax
import jax.numpy as jnp
from jax import lax
from jax.experimental import pallas as pl
from jax.experimental.pallas import tpu as pltpu

F32 = jnp.float32
BF16 = jnp.bfloat16

D_MODEL = 1024
PAST_LEN = 16384
N_BRANCH = 4
BRANCH_WIDTH = D_MODEL // 2
SC_CONV_W = 3
CF_CONV_W = 31
N_HEADS = 8
N_KV_HEADS = 2
HEAD_DIM = 64
GROUP = N_HEADS // N_KV_HEADS
WINDOW = 128
ATTN_BLOCK = 128
ROPE_THETA = 10000.0
GLA_HEADS = 4
GLA_DK = 64
GLA_DV = BRANCH_WIDTH // GLA_HEADS
GLA_GATE_RANK = 16
GLA_GATE_NORM = 16.0
GLA_CHUNK = 64
D_FF = -(-8 * D_MODEL // (3 * 256)) * 256
NEG_INF = -1e30
RMS_EPS = 1e-6
LN_EPS = 1e-5

LANES = 128
FF_CHUNK = 256
VMEM_LIMIT = 56 * 1024 * 1024

OFF_AB = 0
W_AB = 5 * BRANCH_WIDTH
OFF_Q = W_AB
OFF_K = OFF_Q + N_HEADS * HEAD_DIM
OFF_V = OFF_K + N_KV_HEADS * HEAD_DIM
OFF_GQ = OFF_V + N_KV_HEADS * HEAD_DIM
OFF_GK = OFF_GQ + GLA_HEADS * GLA_DK
OFF_GV = OFF_GK + GLA_HEADS * GLA_DK
OFF_GG = OFF_GV + GLA_HEADS * GLA_DV
OFF_GA = OFF_GG + GLA_HEADS * GLA_DV
OFF_GATES = OFF_GA + GLA_GATE_RANK
CD_WIDTHS = (768, 512, 512, 512, 128)


def _resident(shape):
    return pl.BlockSpec(shape, lambda *_: (0,) * len(shape), pipeline_mode=pl.Buffered(1))


def _params(n_axes):
    return pltpu.CompilerParams(dimension_semantics=("arbitrary",) * n_axes, vmem_limit_bytes=VMEM_LIMIT)


def _rms(x, g):
    return x * lax.rsqrt(jnp.mean(x * x, axis=-1, keepdims=True) + RMS_EPS) * g


def _dot(a, b):
    return jnp.dot(a, b, preferred_element_type=F32)


def _dot_nt(a, b):
    return lax.dot_general(a, b, (((1,), (1,)), ((), ())), preferred_element_type=F32)


def _dot_tn(a, b):
    return lax.dot_general(a, b, (((0,), (0,)), ((), ())), preferred_element_type=F32)


def _log_sigmoid(z):
    return -(jnp.maximum(-z, 0.0) + jnp.log1p(jnp.exp(-jnp.abs(z))))


def _silu(x):
    return x * jax.nn.sigmoid(x)


def _inproj_body(x_ref, g_ref, w_ref, *out_refs, widths):
    xn = _rms(x_ref[...], g_ref[...]).astype(BF16)
    off = 0
    for o_ref, wd in zip(out_refs, widths):
        o_ref[...] = _dot(xn, w_ref[:, off:off + wd])
        off += wd


def _inproj(x2d, g, w, widths, tm):
    n = x2d.shape[0]
    return pl.pallas_call(
        functools.partial(_inproj_body, widths=widths),
        grid=(n // tm,),
        in_specs=[pl.BlockSpec((tm, D_MODEL), lambda i: (i, 0)), _resident(g.shape), _resident(w.shape)],
        out_specs=[pl.BlockSpec((tm, wd), lambda i: (i, 0)) for wd in widths],
        out_shape=[jax.ShapeDtypeStruct((n, wd), F32) for wd in widths],
        compiler_params=_params(1),
        name="inproj",
    )(x2d, g, w)


def _swa_body(qkv_ref, cos_ref, sin_ref, qn_ref, kn_ref, sink_ref, oc_ref, kc_ref, vc_ref, kprev, vprev):
    j = pl.program_id(1)
    tq = ATTN_BLOCK

    @pl.when(j == 0)
    def _():
        kprev[...] = jnp.zeros_like(kprev)
        vprev[...] = jnp.zeros_like(vprev)

    blk = qkv_ref[...]
    cos = cos_ref[...]
    sin = sin_ref[...]
    lane = lax.broadcasted_iota(jnp.int32, (1, LANES), 1)
    left = (lane // 32) % 2 == 0
    lo64 = lane < 64

    def norm_rope(x, g):
        x2 = x * x
        ssl = jnp.sum(jnp.where(left, x2, 0.0), axis=-1, keepdims=True)
        ssr = jnp.sum(jnp.where(left, 0.0, x2), axis=-1, keepdims=True)
        r = jnp.where(left, lax.rsqrt(ssl * (1.0 / HEAD_DIM) + RMS_EPS), lax.rsqrt(ssr * (1.0 / HEAD_DIM) + RMS_EPS))
        y = x * r * g
        return y * cos + pltpu.roll(y, 64, 1) * sin

    khat = norm_rope(blk[:, 512:640], kn_ref[...])
    v = blk[:, 640:768]
    keys = jnp.concatenate([kprev[...], khat], axis=0).astype(BF16)
    vals = jnp.concatenate([vprev[...], v], axis=0).astype(BF16)

    t_idx = lax.broadcasted_iota(jnp.int32, (tq, 2 * tq), 0)
    s_idx = lax.broadcasted_iota(jnp.int32, (tq, 2 * tq), 1)
    valid = ((s_idx < tq) & (s_idx > t_idx) & (j > 0)) | ((s_idx >= tq) & ((s_idx - tq) <= t_idx))
    bias = jnp.where(valid, 0.0, NEG_INF)

    for p in range(GROUP):
        qhat = norm_rope(blk[:, p * LANES:(p + 1) * LANES], qn_ref[...])
        outs = []
        for side in range(2):
            qm = jnp.where(left == (side == 0), qhat, 0.0).astype(BF16)
            s = _dot_nt(qm, keys) * (HEAD_DIM ** -0.5) + bias
            sink = sink_ref[p + GROUP * side:p + GROUP * side + 1, 0:1]
            m = jnp.maximum(jnp.max(s, axis=-1, keepdims=True), sink)
            e = jnp.exp(s - m)
            denom = jnp.sum(e, axis=-1, keepdims=True) + jnp.exp(sink - m)
            outs.append(_dot((e / denom).astype(BF16), vals))
        oc_ref[:, p * LANES:(p + 1) * LANES] = jnp.where(lo64, outs[0], outs[1])

    kprev[...] = khat
    vprev[...] = v
    kc_ref[...] = khat
    vc_ref[...] = v


def _swa_prompt(qkv, cos, sin, qn, kn, sinks, batch, seq):
    nb = seq // ATTN_BLOCK
    row = lambda b, j: (b * nb + j, 0)
    return pl.pallas_call(
        _swa_body,
        grid=(batch, nb),
        in_specs=[pl.BlockSpec((ATTN_BLOCK, 768), row),
                  pl.BlockSpec((ATTN_BLOCK, LANES), lambda b, j: (j, 0)),
                  pl.BlockSpec((ATTN_BLOCK, LANES), lambda b, j: (j, 0)),
                  _resident(qn.shape), _resident(kn.shape), _resident(sinks.shape)],
        out_specs=[pl.BlockSpec((ATTN_BLOCK, 512), row),
                   pl.BlockSpec((ATTN_BLOCK, LANES), lambda b, j: (b, 0)),
                   pl.BlockSpec((ATTN_BLOCK, LANES), lambda b, j: (b, 0))],
        out_shape=[jax.ShapeDtypeStruct((batch * seq, 512), F32),
                   jax.ShapeDtypeStruct((batch * WINDOW, LANES), F32),
                   jax.ShapeDtypeStruct((batch * WINDOW, LANES), F32)],
        scratch_shapes=[pltpu.VMEM((ATTN_BLOCK, LANES), F32), pltpu.VMEM((ATTN_BLOCK, LANES), F32)],
        compiler_params=_params(2),
        name="swa_prompt",
    )(qkv, cos, sin, qn, kn, sinks)


def _split3(x):
    hi = x.astype(BF16)
    r1 = x - hi.astype(F32)
    mid = r1.astype(BF16)
    lo = (r1 - mid.astype(F32)).astype(BF16)
    return hi, mid, lo


def _gla_body(gqk_ref, gv_ref, gg_ref, ga_ref, wg_ref, bg_ref, gn_ref, od_ref, st_ref, st_scr, *, tg):
    t = pl.program_id(1)
    ck = GLA_CHUNK

    @pl.when(t == 0)
    def _():
        st_scr[...] = jnp.zeros_like(st_scr)

    z = _dot(ga_ref[...].astype(BF16), wg_ref[...]) + bg_ref[...]
    la = _log_sigmoid(z) * (1.0 / GLA_GATE_NORM)
    r_i = lax.broadcasted_iota(jnp.int32, (tg, tg), 0)
    c_i = lax.broadcasted_iota(jnp.int32, (tg, tg), 1)
    same = (r_i // ck) == (c_i // ck)
    tri = jnp.where(same & (c_i <= r_i), 1.0, 0.0).astype(BF16)
    ones = jnp.where(same, 1.0, 0.0).astype(BF16)
    parts = _split3(la)
    bcum = _dot(tri, parts[0]) + _dot(tri, parts[1]) + _dot(tri, parts[2])
    btot = _dot(ones, parts[0]) + _dot(ones, parts[1]) + _dot(ones, parts[2])

    gqk = gqk_ref[...]
    nk = GLA_HEADS * GLA_DK
    gk = gqk[:, nk:]
    qe = (gqk[:, :nk] * (GLA_DK ** -0.5)) * jnp.exp(bcum)
    ke = gk * jnp.exp(-bcum)
    kd = gk * jnp.exp(btot - bcum)
    dec = jnp.exp(btot)

    lane = lax.broadcasted_iota(jnp.int32, (1, LANES), 1)
    lo64 = lane < 64
    causal = lax.broadcasted_iota(jnp.int32, (ck, ck), 1) <= lax.broadcasted_iota(jnp.int32, (ck, ck), 0)
    u_row = lax.broadcasted_iota(jnp.int32, (2 * GLA_DV, LANES), 0)
    u_lane = lax.broadcasted_iota(jnp.int32, (2 * GLA_DV, LANES), 1)
    diag_blocks = (u_row < GLA_DV) == (u_lane < GLA_DK)
    gn = gn_ref[...]

    for c in range(tg // ck):
        rows = slice(c * ck, (c + 1) * ck)
        for p in range(GLA_HEADS // 2):
            lanes = slice(p * LANES, (p + 1) * LANES)
            vcols = slice(p * 2 * GLA_DV, (p + 1) * 2 * GLA_DV)
            qe_p = qe[rows, lanes]
            ke_p = ke[rows, lanes].astype(BF16)
            kd_p = kd[rows, lanes].astype(BF16)
            v_p = gv_ref[rows, vcols].astype(BF16)
            st = st_scr[p]
            o_halves = []
            for side in range(2):
                qm = jnp.where(lo64 == (side == 0), qe_p, 0.0).astype(BF16)
                a = jnp.where(causal, _dot_nt(qm, ke_p), 0.0).astype(BF16)
                o_halves.append(_dot(a, v_p[:, side * GLA_DV:(side + 1) * GLA_DV]))
            o = jnp.concatenate(o_halves, axis=1) + _dot_nt(qe_p.astype(BF16), st.astype(BF16))
            upd = jnp.where(diag_blocks, _dot_tn(v_p, kd_p), 0.0)
            st_scr[p] = dec[c * ck:c * ck + 1, lanes] * st + upd
            gate = gg_ref[rows, vcols]
            for side in range(2):
                hs = slice(side * GLA_DV, (side + 1) * GLA_DV)
                od_ref[rows, p * 2 * GLA_DV + side * GLA_DV:p * 2 * GLA_DV + (side + 1) * GLA_DV] = (
                    _rms(o[:, hs], gn) * _silu(gate[:, hs]))

    st_ref[...] = st_scr[...]


def _gla_prompt(gqk, gv, gg, ga, wg, bg, gn, batch, seq, tg):
    nt = seq // tg
    row = lambda b, t: (b * nt + t, 0)
    return pl.pallas_call(
        functools.partial(_gla_body, tg=tg),
        grid=(batch, nt),
        in_specs=[pl.BlockSpec((tg, 512), row), pl.BlockSpec((tg, 512), row), pl.BlockSpec((tg, 512), row),
                  pl.BlockSpec((tg, LANES), row), _resident(wg.shape), _resident(bg.shape), _resident(gn.shape)],
        out_specs=[pl.BlockSpec((tg, 512), row),
                   pl.BlockSpec((None, 2, 2 * GLA_DV, LANES), lambda b, t: (b, 0, 0, 0))],
        out_shape=[jax.ShapeDtypeStruct((batch * seq, 512), F32),
                   jax.ShapeDtypeStruct((batch, 2, 2 * GLA_DV, LANES), F32)],
        scratch_shapes=[pltpu.VMEM((2, 2 * GLA_DV, LANES), F32)],
        compiler_params=_params(2),
        name="gla_prompt",
    )(gqk, gv, gg, ga, wg, bg, gn)


U_PAD = 8
G_PAD = 32
CONV_ROWS = 32


def _convs_body(x_ref, g_ref, w_ref, cs_ref, cc_ref, ccb_ref, lng_ref, lnb_ref,
                oa_ref, ob_ref, sc_ref, cf_ref, ubuf, gbuf, cbuf, *, tm):
    i = pl.program_id(1)
    bw = BRANCH_WIDTH

    @pl.when(i == 0)
    def _():
        ubuf[0:U_PAD, :] = jnp.zeros((U_PAD, bw), F32)
        gbuf[0:G_PAD, :] = jnp.zeros((G_PAD, bw), F32)

    @pl.when(i > 0)
    def _():
        ubuf[0:U_PAD, :] = ubuf[tm:tm + U_PAD, :]
        gbuf[0:G_PAD, :] = gbuf[tm:tm + G_PAD, :]

    xn = _rms(x_ref[...], g_ref[...]).astype(BF16)
    sc_h = _dot(xn, w_ref[:, 0:bw])
    sc_c = _dot(xn, w_ref[:, 2 * bw:3 * bw])
    u = sc_c * sc_h
    ubuf[U_PAD:U_PAD + tm, :] = u
    conv_u = (cs_ref[0:1, :] * ubuf[U_PAD - 2:U_PAD - 2 + tm, :] + cs_ref[1:2, :] * ubuf[U_PAD - 1:U_PAD - 1 + tm, :]
              + cs_ref[2:3, :] * u)
    oa_ref[...] = _dot(xn, w_ref[:, bw:2 * bw]) * conv_u

    cf_a = _dot(xn, w_ref[:, 3 * bw:4 * bw])
    cf_b = _dot(xn, w_ref[:, 4 * bw:5 * bw])
    gbuf[G_PAD:G_PAD + tm, :] = cf_a * jax.nn.sigmoid(cf_b)
    base = G_PAD - (CF_CONV_W - 1)
    for r0 in range(0, tm, CONV_ROWS):
        acc = cc_ref[0:1, :] * gbuf[base + r0:base + r0 + CONV_ROWS, :]
        for k in range(1, CF_CONV_W):
            acc = acc + cc_ref[k:k + 1, :] * gbuf[base + r0 + k:base + r0 + k + CONV_ROWS, :]
        cbuf[r0:r0 + CONV_ROWS, :] = acc
    c = cbuf[...] + ccb_ref[...]
    mu = jnp.mean(c, axis=-1, keepdims=True)
    xc = c - mu
    y = xc * lax.rsqrt(jnp.mean(xc * xc, axis=-1, keepdims=True) + LN_EPS) * lng_ref[...] + lnb_ref[...]
    ob_ref[...] = _silu(y)

    sc_ref[...] = ubuf[tm:tm + U_PAD, :]
    cf_ref[...] = gbuf[tm:tm + G_PAD, :]


def _convs_prompt(x2d, g, w_ab, cs, cc, ccb, lng, lnb, batch, seq, tm):
    nt = seq // tm
    row = lambda b, i: (b * nt + i, 0)
    bw = BRANCH_WIDTH
    return pl.pallas_call(
        functools.partial(_convs_body, tm=tm),
        grid=(batch, nt),
        in_specs=[pl.BlockSpec((tm, D_MODEL), row), _resident(g.shape), _resident(w_ab.shape), _resident(cs.shape),
                  _resident(cc.shape), _resident(ccb.shape), _resident(lng.shape), _resident(lnb.shape)],
        out_specs=[pl.BlockSpec((tm, bw), row), pl.BlockSpec((tm, bw), row),
                   pl.BlockSpec((U_PAD, bw), lambda b, i: (b, 0)), pl.BlockSpec((G_PAD, bw), lambda b, i: (b, 0))],
        out_shape=[jax.ShapeDtypeStruct((batch * seq, bw), F32), jax.ShapeDtypeStruct((batch * seq, bw), F32),
                   jax.ShapeDtypeStruct((batch * U_PAD, bw), F32), jax.ShapeDtypeStruct((batch * G_PAD, bw), F32)],
        scratch_shapes=[pltpu.VMEM((U_PAD + tm, bw), F32), pltpu.VMEM((G_PAD + tm, bw), F32), pltpu.VMEM((tm, bw), F32)],
        compiler_params=_params(2),
        name="convs_prompt",
    )(x2d, g, w_ab, cs, cc, ccb, lng, lnb)


def _merge_ffn_body(x_ref, oa_ref, ob_ref, oc_ref, od_ref, gm_ref, wgt_ref, wbr_ref, wout_ref,
                    gf_ref, wfi_ref, wfo_ref, y_ref):
    x = x_ref[...]
    xn = _rms(x, gm_ref[...]).astype(BF16)
    merged = None
    for r, o_ref in enumerate((oa_ref, ob_ref, oc_ref, od_ref)):
        branch = _dot(o_ref[...].astype(BF16), wbr_ref[r])
        gate = jax.nn.sigmoid(_dot(xn, wgt_ref[:, r * D_MODEL:(r + 1) * D_MODEL]))
        merged = gate * branch if merged is None else merged + gate * branch
    x1 = x + _dot(merged.astype(BF16), wout_ref[...])
    xn2 = _rms(x1, gf_ref[...]).astype(BF16)
    y = x1
    for c0 in range(0, D_FF, FF_CHUNK):
        h_gate = _dot(xn2, wfi_ref[:, c0:c0 + FF_CHUNK])
        h_up = _dot(xn2, wfi_ref[:, D_FF + c0:D_FF + c0 + FF_CHUNK])
        y = y + _dot((_silu(h_gate) * h_up).astype(BF16), wfo_ref[c0:c0 + FF_CHUNK, :])
    y_ref[...] = y


def _merge_ffn(x2d, oa, ob, oc, od, gm, wgt, wbr, wout, gf, wfi, wfo, tm):
    n = x2d.shape[0]
    bw = BRANCH_WIDTH
    row = lambda i: (i, 0)
    return pl.pallas_call(
        _merge_ffn_body,
        grid=(n // tm,),
        in_specs=[pl.BlockSpec((tm, D_MODEL), row)] + [pl.BlockSpec((tm, bw), row)] * 4
                 + [_resident(a.shape) for a in (gm, wgt, wbr, wout, gf, wfi, wfo)],
        out_specs=pl.BlockSpec((tm, D_MODEL), row),
        out_shape=jax.ShapeDtypeStruct((n, D_MODEL), F32),
        compiler_params=_params(1),
        name="merge_ffn",
    )(x2d, oa, ob, oc, od, gm, wgt, wbr, wout, gf, wfi, wfo)


def _sample_body(ab_ref, qkv_ref, gqk_ref, gv_ref, gg_ref, ga_ref, sct_ref, cft_ref, kc_ref, vc_ref, s_ref,
                 cs_ref, cc_ref, ccb_ref, lng_ref, lnb_ref, qn_ref, kn_ref, cos_ref, sin_ref, sink_ref,
                 wg_ref, bg_ref, gn_ref,
                 oa_ref, ob_ref, oc_ref, od_ref, u_ref, g_ref, kh_ref, sn_ref,
                 qh_scr, gl_scr, *, sb):
    bw = BRANCH_WIDTH
    sc_h = ab_ref[:, 0:bw]
    sc_b = ab_ref[:, bw:2 * bw]
    sc_c = ab_ref[:, 2 * bw:3 * bw]
    u = sc_c * sc_h
    u_ref[...] = u
    oa_ref[...] = sc_b * (cs_ref[0:1, :] * sct_ref[0] + cs_ref[1:2, :] * sct_ref[1] + cs_ref[2:3, :] * u)
    g = ab_ref[:, 3 * bw:4 * bw] * jax.nn.sigmoid(ab_ref[:, 4 * bw:5 * bw])
    g_ref[...] = g
    c = ccb_ref[...] + cc_ref[CF_CONV_W - 1:CF_CONV_W, :] * g
    for k in range(CF_CONV_W - 1):
        c = c + cc_ref[k:k + 1, :] * cft_ref[k]
    mu = jnp.mean(c, axis=-1, keepdims=True)
    xc = c - mu
    ob_ref[...] = _silu(xc * lax.rsqrt(jnp.mean(xc * xc, axis=-1, keepdims=True) + LN_EPS) * lng_ref[...] + lnb_ref[...])

    lane = lax.broadcasted_iota(jnp.int32, (1, LANES), 1)
    lo64 = lane < 64
    first32 = (lane % 64) < 32
    cos = cos_ref[...]
    sin = sin_ref[...]

    def norm_rope(x, gw):
        x2 = x * x
        ssl = jnp.sum(jnp.where(lo64, x2, 0.0), axis=-1, keepdims=True)
        ssr = jnp.sum(jnp.where(lo64, 0.0, x2), axis=-1, keepdims=True)
        r = jnp.where(lo64, lax.rsqrt(ssl * (1.0 / HEAD_DIM) + RMS_EPS), lax.rsqrt(ssr * (1.0 / HEAD_DIM) + RMS_EPS))
        y = x * r * gw
        partner = jnp.where(first32, pltpu.roll(y, LANES - 32, 1), pltpu.roll(y, 32, 1))
        return y * cos + partner * sin

    for p in range(GROUP):
        qh_scr[:, p * LANES:(p + 1) * LANES] = norm_rope(qkv_ref[:, p * LANES:(p + 1) * LANES], qn_ref[...])
    khat = norm_rope(qkv_ref[:, 512:640], kn_ref[...])
    kh_ref[...] = khat
    qh_scr[:, 512:640] = khat

    z = _dot(ga_ref[...].astype(BF16), wg_ref[...]) + bg_ref[...]
    nk = GLA_HEADS * GLA_DK
    gl_scr[:, 0:nk] = jnp.exp(_log_sigmoid(z) * (1.0 / GLA_GATE_NORM))
    gl_scr[:, nk:2 * nk] = gqk_ref[:, 0:nk] * (GLA_DK ** -0.5)
    gl_scr[:, 2 * nk:3 * nk] = gqk_ref[:, nk:2 * nk]

    row_i = lax.broadcasted_iota(jnp.int32, (LANES, LANES), 0)
    eye = row_i == lax.broadcasted_iota(jnp.int32, (LANES, LANES), 1)
    top = row_i < GLA_DK
    key_bias = jnp.where(lax.broadcasted_iota(jnp.int32, (WINDOW, 1), 0) == 0, NEG_INF, 0.0)

    def to_col(row):
        return jnp.sum(jnp.where(eye, jnp.broadcast_to(row, (LANES, LANES)), 0.0), axis=-1, keepdims=True)

    for b in range(sb):
        kmat = kc_ref[b]
        vmat = vc_ref[b]
        k_new = qh_scr[b:b + 1, 512:640]
        v_new = qkv_ref[b:b + 1, 640:768]
        for p in range(GROUP):
            q_row = qh_scr[b:b + 1, p * LANES:(p + 1) * LANES]
            prod = kmat * q_row
            pn = k_new * q_row
            outs = []
            for side in range(2):
                sel = lo64 == (side == 0)
                s = jnp.sum(jnp.where(sel, prod, 0.0), axis=-1, keepdims=True) * (HEAD_DIM ** -0.5) + key_bias
                s_n = jnp.sum(jnp.where(sel, pn, 0.0), axis=-1, keepdims=True) * (HEAD_DIM ** -0.5)
                sink = sink_ref[p + GROUP * side:p + GROUP * side + 1, 0:1]
                m = jnp.maximum(jnp.maximum(jnp.max(s, axis=0, keepdims=True), s_n), sink)
                e = jnp.exp(s - m)
                e_n = jnp.exp(s_n - m)
                denom = jnp.sum(e, axis=0, keepdims=True) + e_n + jnp.exp(sink - m)
                outs.append((jnp.sum(e * vmat, axis=0, keepdims=True) + e_n * v_new) / denom)
            oc_ref[b:b + 1, p * LANES:(p + 1) * LANES] = jnp.where(lo64, outs[0], outs[1])

        for p in range(GLA_HEADS // 2):
            lanes = slice(p * LANES, (p + 1) * LANES)
            a_col = to_col(gl_scr[b:b + 1, p * LANES:(p + 1) * LANES])
            q_col = to_col(gl_scr[b:b + 1, nk + p * LANES:nk + (p + 1) * LANES])
            k_col = to_col(gl_scr[b:b + 1, 2 * nk + p * LANES:2 * nk + (p + 1) * LANES])
            v0 = gv_ref[b:b + 1, 2 * p * GLA_DV:(2 * p + 1) * GLA_DV]
            v1 = gv_ref[b:b + 1, (2 * p + 1) * GLA_DV:(2 * p + 2) * GLA_DV]
            vsel = jnp.where(top, jnp.broadcast_to(v0, (LANES, GLA_DV)), jnp.broadcast_to(v1, (LANES, GLA_DV)))
            s_new = a_col * s_ref[b, lanes, :] + k_col * vsel
            sn_ref[b, lanes, :] = s_new
            w = q_col * s_new
            od_ref[b:b + 1, 2 * p * GLA_DV:(2 * p + 1) * GLA_DV] = jnp.sum(w[0:GLA_DK], axis=0, keepdims=True)
            od_ref[b:b + 1, (2 * p + 1) * GLA_DV:(2 * p + 2) * GLA_DV] = jnp.sum(w[GLA_DK:], axis=0, keepdims=True)

    gn = gn_ref[...]
    for h in range(GLA_HEADS):
        hs = slice(h * GLA_DV, (h + 1) * GLA_DV)
        od_ref[:, hs] = _rms(od_ref[:, hs], gn) * _silu(gg_ref[:, hs])


def _sample_mixers(ab, qkv, gqk, gv, gg, ga, sct, cft, kc, vc, s0, small, sb):
    n = ab.shape[0]
    bw = BRANCH_WIDTH
    row = lambda i: (i, 0)
    lead3 = lambda i: (i, 0, 0)
    mid3 = lambda i: (0, i, 0)
    return pl.pallas_call(
        functools.partial(_sample_body, sb=sb),
        grid=(n // sb,),
        in_specs=[pl.BlockSpec((sb, W_AB), row), pl.BlockSpec((sb, 768), row), pl.BlockSpec((sb, 512), row),
                  pl.BlockSpec((sb, 512), row), pl.BlockSpec((sb, 512), row), pl.BlockSpec((sb, LANES), row),
                  pl.BlockSpec((SC_CONV_W - 1, sb, bw), mid3), pl.BlockSpec((CF_CONV_W - 1, sb, bw), mid3),
                  pl.BlockSpec((sb, WINDOW, LANES), lead3), pl.BlockSpec((sb, WINDOW, LANES), lead3),
                  pl.BlockSpec((sb, GLA_HEADS * GLA_DK, GLA_DV), lead3)]
                 + [_resident(a.shape) for a in small],
        out_specs=[pl.BlockSpec((sb, bw), row)] * 6 + [pl.BlockSpec((sb, LANES), row),
                                                       pl.BlockSpec((sb, GLA_HEADS * GLA_DK, GLA_DV), lead3)],
        out_shape=[jax.ShapeDtypeStruct((n, bw), F32)] * 6 + [jax.ShapeDtypeStruct((n, LANES), F32),
                                                              jax.ShapeDtypeStruct(s0.shape, F32)],
        scratch_shapes=[pltpu.VMEM((sb, 640), F32), pltpu.VMEM((sb, 3 * GLA_HEADS * GLA_DK), F32)],
        compiler_params=_params(1),
        name="sample_mixers",
    )(ab, qkv, gqk, gv, gg, ga, sct, cft, kc, vc, s0, *small)


def _qk_layout(rope_split):
    q_cols, k_cols, dims = [], [], []
    for lane in range(LANES):
        if rope_split:
            quarter, i = divmod(lane, 32)
            side, d = quarter % 2, i + 32 * (quarter // 2)
        else:
            side, d = divmod(lane, 64)
        dims.append(d)
        k_cols.append(side * HEAD_DIM + d)
    for p in range(GROUP):
        for lane in range(LANES):
            side = (k_cols[lane] // HEAD_DIM)
            q_cols.append((p + GROUP * side) * HEAD_DIM + dims[lane])
    return np.array(q_cols), np.array(k_cols), np.array(dims)


def _rope_tables(pos, dims):
    half = HEAD_DIM // 2
    inv = ROPE_THETA ** (-jnp.arange(half, dtype=F32) / half)
    ang = pos.astype(F32)[:, None] * inv[None, :]
    cos = jnp.cos(ang)[:, dims % half]
    sin = jnp.sin(ang)[:, dims % half] * jnp.where(dims < half, -1.0, 1.0).astype(F32)[None, :]
    return cos, sin


def _cd_weight(w_in, q_cols, k_cols):
    pad = jnp.zeros((D_MODEL, LANES - GLA_GATE_RANK), w_in.dtype)
    return jnp.concatenate([w_in[:, OFF_Q + q_cols], w_in[:, OFF_K + k_cols], w_in[:, OFF_V:OFF_GA], w_in[:, OFF_GA:OFF_GATES],
                            pad], axis=1)


def kernel(x_prompt, x_sample, cache_swa_k, cache_swa_v, state_sconv, state_cconv, state_gla, norm_mix, w_in,
           conv_short, conv_conf, conv_conf_b, conf_ln_g, conf_ln_b, q_norm, k_norm, attn_sinks, w_gla_gate,
           b_gla_gate, gla_norm, w_branch, w_out, norm_ffn, w_ffn_in, w_ffn_out):
    batch, seq, _ = x_prompt.shape
    n_dec = x_sample.shape[0]
    depth = w_in.shape[0]
    bw = BRANCH_WIDTH

    q_cols_p, k_cols_p, dims_p = _qk_layout(True)
    q_cols_s, k_cols_s, dims_s = _qk_layout(False)
    cos_p, sin_p = _rope_tables(jnp.arange(seq), dims_p)
    cos_s, sin_s = _rope_tables(PAST_LEN + jnp.arange(1), dims_s)
    k_unperm = np.argsort(k_cols_p)
    oc_rows = np.array([(p + GROUP * side) * HEAD_DIM + d for p in range(GROUP) for side in range(2) for d in range(HEAD_DIM)])

    yp = x_prompt.reshape(batch * seq, D_MODEL)
    ys = x_sample.reshape(n_dec, D_MODEL)
    outs = {k: [] for k in ("sc_p", "sc_s", "cf_p", "cf_s", "k_p", "k_s", "v_p", "v_s", "S_p", "S_s")}
    row2 = lambda a: a.reshape(1, -1)

    for l in range(depth):
        wl = w_in[l]
        w_ab = wl[:, :W_AB].astype(BF16)
        w_cd_p = _cd_weight(wl, q_cols_p, k_cols_p).astype(BF16)
        w_s = jnp.concatenate([wl[:, :W_AB], _cd_weight(wl, q_cols_s, k_cols_s)], axis=1).astype(BF16)
        w_gates = wl[:, OFF_GATES:].astype(BF16)
        wbr = jnp.stack([w_branch[l, 0], w_branch[l, 1], w_branch[l, 2][oc_rows], w_branch[l, 3]]).astype(BF16)
        wout = w_out[l].astype(BF16)
        wfi = w_ffn_in[l].astype(BF16)
        wfo = w_ffn_out[l].astype(BF16)
        wg = jnp.concatenate([w_gla_gate[l], jnp.zeros((LANES - GLA_GATE_RANK, GLA_HEADS * GLA_DK), F32)], axis=0).astype(BF16)
        bg = row2(b_gla_gate[l])
        gn = row2(gla_norm[l])
        gm = row2(norm_mix[l])
        gf = row2(norm_ffn[l])
        sinks = jnp.broadcast_to(attn_sinks[l][:, None], (N_HEADS, LANES))
        cs, cc, ccb = conv_short[l], conv_conf[l], row2(conv_conf_b[l])
        lng, lnb = row2(conf_ln_g[l]), row2(conf_ln_b[l])

        qkv, gqk, gv, gg, ga = _inproj(yp, gm, w_cd_p, CD_WIDTHS, 512)
        oc, kc, vc = _swa_prompt(qkv, cos_p, sin_p, row2(q_norm[l][dims_p]), row2(k_norm[l][dims_p]), sinks, batch, seq)
        od, st = _gla_prompt(gqk, gv, gg, ga, wg, bg, gn, batch, seq, 512)
        oa, ob, sc_new, cf_new = _convs_prompt(yp, gm, w_ab, cs, cc, ccb, lng, lnb, batch, seq, 256)
        yp = _merge_ffn(yp, oa, ob, oc, od, gm, w_gates, wbr, wout, gf, wfi, wfo, 256)
        outs["sc_p"].append(sc_new.reshape(batch, U_PAD, bw)[:, U_PAD - (SC_CONV_W - 1):])
        outs["cf_p"].append(cf_new.reshape(batch, G_PAD, bw)[:, G_PAD - (CF_CONV_W - 1):])
        outs["k_p"].append(kc.reshape(batch, WINDOW, LANES)[:, :, k_unperm].reshape(batch, WINDOW, N_KV_HEADS, HEAD_DIM))
        outs["v_p"].append(vc.reshape(batch, WINDOW, N_KV_HEADS, HEAD_DIM))
        st = st.reshape(batch, GLA_HEADS // 2, 2, GLA_DV, 2, GLA_DK)
        outs["S_p"].append(jnp.stack([st[:, :, 0, :, 0], st[:, :, 1, :, 1]], axis=2)
                           .reshape(batch, GLA_HEADS, GLA_DV, GLA_DK).transpose(0, 1, 3, 2))

        ab_s, qkv_s, gqk_s, gv_s, gg_s, ga_s = _inproj(ys, gm, w_s, (W_AB,) + CD_WIDTHS, n_dec)
        small = (cs, cc, ccb, lng, lnb, row2(q_norm[l][dims_s]), row2(k_norm[l][dims_s]), cos_s, sin_s, sinks, wg, bg, gn)
        oa_s, ob_s, oc_s, od_s, u_s, g_s, kh_s, s_new = _sample_mixers(
            ab_s, qkv_s, gqk_s, gv_s, gg_s, ga_s,
            state_sconv[l].transpose(1, 0, 2), state_cconv[l].transpose(1, 0, 2),
            cache_swa_k[l].reshape(n_dec, WINDOW, LANES), cache_swa_v[l].reshape(n_dec, WINDOW, LANES),
            state_gla[l].reshape(n_dec, GLA_HEADS * GLA_DK, GLA_DV), small, 8)
        ys = _merge_ffn(ys, oa_s, ob_s, oc_s, od_s, gm, w_gates, wbr, wout, gf, wfi, wfo, n_dec)
        outs["sc_s"].append(jnp.concatenate([state_sconv[l][:, 1:], u_s[:, None]], axis=1))
        outs["cf_s"].append(jnp.concatenate([state_cconv[l][:, 1:], g_s[:, None]], axis=1))
        outs["k_s"].append(jnp.concatenate(
            [cache_swa_k[l][:, 1:], kh_s.reshape(n_dec, 1, N_KV_HEADS, HEAD_DIM)], axis=1))
        outs["v_s"].append(jnp.concatenate(
            [cache_swa_v[l][:, 1:], qkv_s[:, 640:768].reshape(n_dec, 1, N_KV_HEADS, HEAD_DIM)], axis=1))
        outs["S_s"].append(s_new.reshape(n_dec, GLA_HEADS, GLA_DK, GLA_DV))

    return (yp.reshape(batch, seq, D_MODEL), ys.reshape(n_dec, 1, D_MODEL),
            jnp.stack(outs["sc_p"]), jnp.stack(outs["sc_s"]), jnp.stack(outs["cf_p"]), jnp.stack(outs["cf_s"]),
            jnp.stack(outs["k_p"]), jnp.stack(outs["k_s"]), jnp.stack(outs["v_p"]), jnp.stack(outs["v_s"]),
            jnp.stack(outs["S_p"]), jnp.stack(outs["S_s"]))
```

```python
import functools

import numpy as np
import jax
import jax.numpy as jnp
from jax import lax
from jax.experimental import pallas as pl
from jax.experimental.pallas import tpu as pltpu

F32 = jnp.float32
BF16 = jnp.bfloat16

D_MODEL = 1024
PAST_LEN = 16384
N_BRANCH = 4
BRANCH_WIDTH = D_MODEL // 2
SC_CONV_W = 3
CF_CONV_W = 31
N_HEADS = 8
N_KV_HEADS = 2
HEAD_DIM = 64
GROUP = N_HEADS // N_KV_HEADS
WINDOW = 128
ATTN_BLOCK = 128
ROPE_THETA = 10000.0
GLA_HEADS = 4
GLA_DK = 64
GLA_DV = BRANCH_WIDTH // GLA_HEADS
GLA_GATE_RANK = 16
GLA_GATE_NORM = 16.0
GLA_CHUNK = 64
D_FF = -(-8 * D_MODEL // (3 * 256)) * 256
NEG_INF = -1e30
RMS_EPS = 1e-6
LN_EPS = 1e-5

LANES = 128
SUBLANES = 8
FF_CHUNK = 256
VMEM_LIMIT = 56 * 1024 * 1024

OFF_AB = 0
W_AB = 5 * BRANCH_WIDTH
OFF_Q = W_AB
OFF_K = OFF_Q + N_HEADS * HEAD_DIM
OFF_V = OFF_K + N_KV_HEADS * HEAD_DIM
OFF_GQ = OFF_V + N_KV_HEADS * HEAD_DIM
OFF_GK = OFF_GQ + GLA_HEADS * GLA_DK
OFF_GV = OFF_GK + GLA_HEADS * GLA_DK
OFF_GG = OFF_GV + GLA_HEADS * GLA_DV
OFF_GA = OFF_GG + GLA_HEADS * GLA_DV
OFF_GATES = OFF_GA + GLA_GATE_RANK


def _resident(shape):
    return pl.BlockSpec(shape, lambda *_: (0,) * len(shape), pipeline_mode=pl.Buffered(1))


def _params(n_axes):
    return pltpu.CompilerParams(dimension_semantics=("arbitrary",) * n_axes, vmem_limit_bytes=VMEM_LIMIT)


def _rms(x, g):
    return x * lax.rsqrt(jnp.mean(x * x, axis=-1, keepdims=True) + RMS_EPS) * g


def _dot(a, b):
    return jnp.dot(a, b, preferred_element_type=F32)


def _dot_nt(a, b):
    return lax.dot_general(a, b, (((1,), (1,)), ((), ())), preferred_element_type=F32)


def _dot_tn(a, b):
    return lax.dot_general(a, b, (((0,), (0,)), ((), ())), preferred_element_type=F32)


def _log_sigmoid(z):
    return -(jnp.maximum(-z, 0.0) + jnp.log1p(jnp.exp(-jnp.abs(z))))


def _silu(x):
    return x * jax.nn.sigmoid(x)


def _inproj_body(x_ref, g_ref, *refs, widths):
    w_refs, out_refs = refs[:len(widths)], list(refs[len(widths):])
    xn = _rms(x_ref[...], g_ref[...]).astype(BF16)
    for w_ref, group in zip(w_refs, widths):
        off = 0
        for wd in group:
            out_refs.pop(0)[...] = _dot(xn, w_ref[:, off:off + wd])
            off += wd


def _inproj(x2d, g, weights, widths, tm):
    n = x2d.shape[0]
    flat = [wd for group in widths for wd in group]
    return pl.pallas_call(
        functools.partial(_inproj_body, widths=widths),
        grid=(n // tm,),
        in_specs=[pl.BlockSpec((tm, D_MODEL), lambda i: (i, 0)), _resident(g.shape)] + [_resident(w.shape) for w in weights],
        out_specs=[pl.BlockSpec((tm, wd), lambda i: (i, 0)) for wd in flat],
        out_shape=[jax.ShapeDtypeStruct((n, wd), F32) for wd in flat],
        compiler_params=_params(1),
        name="inproj",
    )(x2d, g, *weights)


def _swa_body(qk_ref, v_ref, cos_ref, sin_ref, qn_ref, kn_ref, sink_ref, oc_ref, kc_ref, vc_ref, kprev, vprev, *, nblk):
    j = pl.program_id(1)
    tq = ATTN_BLOCK

    @pl.when(j == 0)
    def _():
        kprev[...] = jnp.zeros_like(kprev)
        vprev[...] = jnp.zeros_like(vprev)

    blk = qk_ref[...]
    cos = cos_ref[...]
    sin = sin_ref[...]
    lane = lax.broadcasted_iota(jnp.int32, (1, LANES), 1)
    left = (lane // 32) % 2 == 0
    lo64 = lane < 64

    def norm_rope(x, g):
        x2 = x * x
        ssl = jnp.sum(jnp.where(left, x2, 0.0), axis=-1, keepdims=True)
        ssr = jnp.sum(jnp.where(left, 0.0, x2), axis=-1, keepdims=True)
        r = jnp.where(left, lax.rsqrt(ssl * (1.0 / HEAD_DIM) + RMS_EPS), lax.rsqrt(ssr * (1.0 / HEAD_DIM) + RMS_EPS))
        y = x * r * g
        return y * cos + pltpu.roll(y, 64, 1) * sin

    khat = norm_rope(blk[:, 512:640], kn_ref[...])
    v = v_ref[...]
    k_all = jnp.concatenate([kprev[...], khat], axis=0).astype(BF16)
    v_all = jnp.concatenate([vprev[...], v], axis=0).astype(BF16)
    qhat = [norm_rope(blk[:, p * LANES:(p + 1) * LANES], qn_ref[...]) * (HEAD_DIM ** -0.5) for p in range(GROUP)]
    upper = lax.broadcasted_iota(jnp.int32, (tq, tq), 1) > lax.broadcasted_iota(jnp.int32, (tq, tq), 0)
    sink = sink_ref[...][:, :, 0:1]

    for i in range(nblk):
        rows = slice(i * tq, (i + 1) * tq)
        qs = []
        for p in range(GROUP):
            qs.append(jnp.where(left, qhat[p][rows], 0.0))
            qs.append(jnp.where(left, 0.0, qhat[p][rows]))
        q_all = jnp.concatenate(qs, axis=0).astype(BF16)
        s2 = _dot_nt(q_all, k_all[i * tq:(i + 2) * tq]).reshape(N_HEADS, tq, 2 * tq)
        s_prev = s2[:, :, :tq]
        if i == 0:
            s_prev = s_prev + jnp.where(j > 0, 0.0, NEG_INF)
        s = jnp.where(upper, s_prev, s2[:, :, tq:])
        m = jnp.maximum(jnp.max(s, axis=-1, keepdims=True), sink)
        e = jnp.exp(s - m)
        inv = 1.0 / (jnp.sum(e, axis=-1, keepdims=True) + jnp.exp(sink - m))
        e2 = jnp.concatenate([jnp.where(upper, e, 0.0), jnp.where(upper, 0.0, e)], axis=-1).astype(BF16)
        o = _dot(e2.reshape(N_HEADS * tq, 2 * tq), v_all[i * tq:(i + 2) * tq]).reshape(N_HEADS, tq, LANES) * inv
        for p in range(GROUP):
            oc_ref[rows, p * LANES:(p + 1) * LANES] = jnp.where(lo64, o[2 * p], o[2 * p + 1]).astype(oc_ref.dtype)

    last = slice((nblk - 1) * tq, nblk * tq)
    kprev[...] = khat[last]
    vprev[...] = v[last]
    kc_ref[...] = khat[last]
    vc_ref[...] = v[last]


def _swa_prompt(qk, v, cos, sin, qn, kn, sinks, batch, seq, nblk):
    ts = nblk * ATTN_BLOCK
    nb = seq // ts
    row = lambda b, j: (b * nb + j, 0)
    return pl.pallas_call(
        functools.partial(_swa_body, nblk=nblk),
        grid=(batch, nb),
        in_specs=[pl.BlockSpec((ts, 640), row), pl.BlockSpec((ts, LANES), row),
                  pl.BlockSpec((ts, LANES), lambda b, j: (j, 0)),
                  pl.BlockSpec((ts, LANES), lambda b, j: (j, 0)),
                  _resident(qn.shape), _resident(kn.shape), _resident(sinks.shape)],
        out_specs=[pl.BlockSpec((ts, 512), row),
                   pl.BlockSpec((ATTN_BLOCK, LANES), lambda b, j: (b, 0)),
                   pl.BlockSpec((ATTN_BLOCK, LANES), lambda b, j: (b, 0))],
        out_shape=[jax.ShapeDtypeStruct((batch * seq, 512), BF16),
                   jax.ShapeDtypeStruct((batch * WINDOW, LANES), F32),
                   jax.ShapeDtypeStruct((batch * WINDOW, LANES), F32)],
        scratch_shapes=[pltpu.VMEM((ATTN_BLOCK, LANES), F32), pltpu.VMEM((ATTN_BLOCK, LANES), F32)],
        compiler_params=_params(2),
        name="swa_prompt",
    )(qk, v, cos, sin, qn, kn, sinks)


def _split3(x):
    hi = x.astype(BF16)
    r1 = x - hi.astype(F32)
    mid = r1.astype(BF16)
    lo = (r1 - mid.astype(F32)).astype(BF16)
    return hi, mid, lo


def _gla_body(gqk_ref, gv_ref, gg_ref, ga_ref, wg_ref, bg_ref, gn_ref, od_ref, st_ref, st_scr, *, tg):
    t = pl.program_id(1)
    ck = GLA_CHUNK

    @pl.when(t == 0)
    def _():
        st_scr[...] = jnp.zeros_like(st_scr)

    z = _dot(ga_ref[...].astype(BF16), wg_ref[...]) + bg_ref[...]
    la = _log_sigmoid(z) * (1.0 / GLA_GATE_NORM)
    r_i = lax.broadcasted_iota(jnp.int32, (tg, tg), 0)
    c_i = lax.broadcasted_iota(jnp.int32, (tg, tg), 1)
    same = (r_i // ck) == (c_i // ck)
    tri = jnp.where(same & (c_i <= r_i), 1.0, 0.0).astype(BF16)
    ones = jnp.where(same, 1.0, 0.0).astype(BF16)
    parts = _split3(la)
    bcum = _dot(tri, parts[0]) + _dot(tri, parts[1]) + _dot(tri, parts[2])
    btot = _dot(ones, parts[0]) + _dot(ones, parts[1]) + _dot(ones, parts[2])

    gqk = gqk_ref[...]
    nk = GLA_HEADS * GLA_DK
    gk = gqk[:, nk:]
    qe = (gqk[:, :nk] * (GLA_DK ** -0.5)) * jnp.exp(bcum)
    ke = gk * jnp.exp(-bcum)
    kd = gk * jnp.exp(btot - bcum)
    dec = jnp.exp(btot)

    lane = lax.broadcasted_iota(jnp.int32, (1, LANES), 1)
    lo64 = lane < 64
    causal = lax.broadcasted_iota(jnp.int32, (ck, ck), 1) <= lax.broadcasted_iota(jnp.int32, (ck, ck), 0)
    u_row = lax.broadcasted_iota(jnp.int32, (2 * GLA_DV, LANES), 0)
    u_lane = lax.broadcasted_iota(jnp.int32, (2 * GLA_DV, LANES), 1)
    diag_blocks = (u_row < GLA_DV) == (u_lane < GLA_DK)
    gn = gn_ref[...]

    for c in range(tg // ck):
        rows = slice(c * ck, (c + 1) * ck)
        for p in range(GLA_HEADS // 2):
            lanes = slice(p * LANES, (p + 1) * LANES)
            vcols = slice(p * 2 * GLA_DV, (p + 1) * 2 * GLA_DV)
            qe_p = qe[rows, lanes]
            ke_p = ke[rows, lanes].astype(BF16)
            kd_p = kd[rows, lanes].astype(BF16)
            v_p = gv_ref[rows, vcols].astype(BF16)
            st = st_scr[p]
            o_halves = []
            for side in range(2):
                qm = jnp.where(lo64 == (side == 0), qe_p, 0.0).astype(BF16)
                a = jnp.where(causal, _dot_nt(qm, ke_p), 0.0).astype(BF16)
                o_halves.append(_dot(a, v_p[:, side * GLA_DV:(side + 1) * GLA_DV]))
            o = jnp.concatenate(o_halves, axis=1) + _dot_nt(qe_p.astype(BF16), st.astype(BF16))
            upd = jnp.where(diag_blocks, _dot_tn(v_p, kd_p), 0.0)
            st_scr[p] = dec[c * ck:c * ck + 1, lanes] * st + upd
            gate = gg_ref[rows, vcols]
            for side in range(2):
                hs = slice(side * GLA_DV, (side + 1) * GLA_DV)
                od_ref[rows, p * 2 * GLA_DV + side * GLA_DV:p * 2 * GLA_DV + (side + 1) * GLA_DV] = (
                    _rms(o[:, hs], gn) * _silu(gate[:, hs])).astype(od_ref.dtype)

    st_ref[...] = st_scr[...]


def _gla_prompt(gqk, gv, gg, ga, wg, bg, gn, batch, seq, tg):
    nt = seq // tg
    row = lambda b, t: (b * nt + t, 0)
    return pl.pallas_call(
        functools.partial(_gla_body, tg=tg),
        grid=(batch, nt),
        in_specs=[pl.BlockSpec((tg, 512), row), pl.BlockSpec((tg, 512), row), pl.BlockSpec((tg, 512), row),
                  pl.BlockSpec((tg, LANES), row), _resident(wg.shape), _resident(bg.shape), _resident(gn.shape)],
        out_specs=[pl.BlockSpec((tg, 512), row),
                   pl.BlockSpec((None, 2, 2 * GLA_DV, LANES), lambda b, t: (b, 0, 0, 0))],
        out_shape=[jax.ShapeDtypeStruct((batch * seq, 512), BF16),
                   jax.ShapeDtypeStruct((batch, 2, 2 * GLA_DV, LANES), F32)],
        scratch_shapes=[pltpu.VMEM((2, 2 * GLA_DV, LANES), F32)],
        compiler_params=_params(2),
        name="gla_prompt",
    )(gqk, gv, gg, ga, wg, bg, gn)


U_PAD = 8
G_PAD = 32
CONV_ROWS = 32


def _convs_body(x_ref, g_ref, w_ref, cs_ref, cc_ref, ccb_ref, lng_ref, lnb_ref,
                oa_ref, ob_ref, sc_ref, cf_ref, ubuf, gbuf, gsh, cbuf, *, tm):
    i = pl.program_id(1)
    bw = BRANCH_WIDTH

    @pl.when(i == 0)
    def _():
        ubuf[0:U_PAD, :] = jnp.zeros((U_PAD, bw), F32)
        gbuf[0:G_PAD, :] = jnp.zeros((G_PAD, bw), F32)

    @pl.when(i > 0)
    def _():
        ubuf[0:U_PAD, :] = ubuf[tm:tm + U_PAD, :]
        gbuf[0:G_PAD, :] = gbuf[tm:tm + G_PAD, :]

    xn = _rms(x_ref[...], g_ref[...]).astype(BF16)
    sc_h = _dot(xn, w_ref[:, 0:bw])
    sc_c = _dot(xn, w_ref[:, 2 * bw:3 * bw])
    u = sc_c * sc_h
    ubuf[U_PAD:U_PAD + tm, :] = u
    conv_u = (cs_ref[0:1, :] * ubuf[U_PAD - 2:U_PAD - 2 + tm, :] + cs_ref[1:2, :] * ubuf[U_PAD - 1:U_PAD - 1 + tm, :]
              + cs_ref[2:3, :] * u)
    oa_ref[...] = (_dot(xn, w_ref[:, bw:2 * bw]) * conv_u).astype(oa_ref.dtype)

    cf_a = _dot(xn, w_ref[:, 3 * bw:4 * bw])
    cf_b = _dot(xn, w_ref[:, 4 * bw:5 * bw])
    gbuf[G_PAD:G_PAD + tm, :] = cf_a * jax.nn.sigmoid(cf_b)
    base = G_PAD - (CF_CONV_W - 1)
    span = tm + G_PAD - SUBLANES
    for sh in range(1, SUBLANES):
        gsh[sh - 1] = gbuf[sh:sh + span, :]
    for r0 in range(0, tm, CONV_ROWS):
        acc = None
        for k in range(CF_CONV_W):
            al, sh = divmod(base + k, SUBLANES)
            rows = slice(r0 + al * SUBLANES, r0 + al * SUBLANES + CONV_ROWS)
            w_tap = jnp.concatenate([cc_ref[k]] * (CONV_ROWS // SUBLANES), axis=0)
            tap = w_tap * (gbuf[rows, :] if sh == 0 else gsh[sh - 1, rows, :])
            acc = tap if acc is None else acc + tap
        cbuf[r0:r0 + CONV_ROWS, :] = acc
    c = cbuf[...] + ccb_ref[...]
    mu = jnp.mean(c, axis=-1, keepdims=True)
    xc = c - mu
    y = xc * lax.rsqrt(jnp.mean(xc * xc, axis=-1, keepdims=True) + LN_EPS) * lng_ref[...] + lnb_ref[...]
    ob_ref[...] = _silu(y).astype(ob_ref.dtype)

    sc_ref[...] = ubuf[tm:tm + U_PAD, :]
    cf_ref[...] = gbuf[tm:tm + G_PAD, :]


def _convs_prompt(x2d, g, w_ab, cs, cc, ccb, lng, lnb, batch, seq, tm):
    nt = seq // tm
    row = lambda b, i: (b * nt + i, 0)
    bw = BRANCH_WIDTH
    return pl.pallas_call(
        functools.partial(_convs_body, tm=tm),
        grid=(batch, nt),
        in_specs=[pl.BlockSpec((tm, D_MODEL), row), _resident(g.shape), _resident(w_ab.shape), _resident(cs.shape),
                  _resident(cc.shape), _resident(ccb.shape), _resident(lng.shape), _resident(lnb.shape)],
        out_specs=[pl.BlockSpec((tm, bw), row), pl.BlockSpec((tm, bw), row),
                   pl.BlockSpec((U_PAD, bw), lambda b, i: (b, 0)), pl.BlockSpec((G_PAD, bw), lambda b, i: (b, 0))],
        out_shape=[jax.ShapeDtypeStruct((batch * seq, bw), BF16), jax.ShapeDtypeStruct((batch * seq, bw), BF16),
                   jax.ShapeDtypeStruct((batch * U_PAD, bw), F32), jax.ShapeDtypeStruct((batch * G_PAD, bw), F32)],
        scratch_shapes=[pltpu.VMEM((U_PAD + tm, bw), F32), pltpu.VMEM((G_PAD + tm, bw), F32),
                        pltpu.VMEM((SUBLANES - 1, tm + G_PAD - SUBLANES, bw), F32), pltpu.VMEM((tm, bw), F32)],
        compiler_params=_params(2),
        name="convs_prompt",
    )(x2d, g, w_ab, cs, cc, ccb, lng, lnb)


def _merge_ffn_body(x_ref, oa_ref, ob_ref, oc_ref, od_ref, gm_ref, wgt_ref, wbr_ref, wout_ref,
                    gf_ref, wfi_ref, wfo_ref, y_ref):
    x = x_ref[...]
    xn = _rms(x, gm_ref[...]).astype(BF16)
    merged = None
    for r, o_ref in enumerate((oa_ref, ob_ref, oc_ref, od_ref)):
        branch = _dot(o_ref[...].astype(BF16), wbr_ref[r])
        gate = jax.nn.sigmoid(_dot(xn, wgt_ref[:, r * D_MODEL:(r + 1) * D_MODEL]))
        merged = gate * branch if merged is None else merged + gate * branch
    x1 = x + _dot(merged.astype(BF16), wout_ref[...])
    xn2 = _rms(x1, gf_ref[...]).astype(BF16)
    y = x1
    for c0 in range(0, D_FF, FF_CHUNK):
        h_gate = _dot(xn2, wfi_ref[:, c0:c0 + FF_CHUNK])
        h_up = _dot(xn2, wfi_ref[:, D_FF + c0:D_FF + c0 + FF_CHUNK])
        y = y + _dot((_silu(h_gate) * h_up).astype(BF16), wfo_ref[c0:c0 + FF_CHUNK, :])
    y_ref[...] = y


def _merge_ffn(x2d, oa, ob, oc, od, gm, wgt, wbr, wout, gf, wfi, wfo, tm):
    n = x2d.shape[0]
    bw = BRANCH_WIDTH
    row = lambda i: (i, 0)
    return pl.pallas_call(
        _merge_ffn_body,
        grid=(n // tm,),
        in_specs=[pl.BlockSpec((tm, D_MODEL), row)] + [pl.BlockSpec((tm, bw), row)] * 4
                 + [_resident(a.shape) for a in (gm, wgt, wbr, wout, gf, wfi, wfo)],
        out_specs=pl.BlockSpec((tm, D_MODEL), row),
        out_shape=jax.ShapeDtypeStruct((n, D_MODEL), F32),
        compiler_params=_params(1),
        name="merge_ffn",
    )(x2d, oa, ob, oc, od, gm, wgt, wbr, wout, gf, wfi, wfo)


def _sample_body(ab_ref, qk_ref, v_ref, gqk_ref, gv_ref, gg_ref, ga_ref, sct_ref, cft_ref, kc_ref, vc_ref, s_ref,
                 cs_ref, cc_ref, ccb_ref, lng_ref, lnb_ref, qn_ref, kn_ref, cos_ref, sin_ref, sink_ref,
                 wg_ref, bg_ref, gn_ref,
                 oa_ref, ob_ref, oc_ref, od_ref, u_ref, g_ref, kh_ref, sn_ref,
                 qh_scr, gl_scr, *, sb):
    bw = BRANCH_WIDTH
    sc_h = ab_ref[:, 0:bw]
    sc_b = ab_ref[:, bw:2 * bw]
    sc_c = ab_ref[:, 2 * bw:3 * bw]
    u = sc_c * sc_h
    u_ref[...] = u
    oa_ref[...] = sc_b * (cs_ref[0:1, :] * sct_ref[0] + cs_ref[1:2, :] * sct_ref[1] + cs_ref[2:3, :] * u)
    g = ab_ref[:, 3 * bw:4 * bw] * jax.nn.sigmoid(ab_ref[:, 4 * bw:5 * bw])
    g_ref[...] = g
    c = ccb_ref[...] + cc_ref[CF_CONV_W - 1, 0:1, :] * g
    for k in range(CF_CONV_W - 1):
        c = c + cc_ref[k, 0:1, :] * cft_ref[k]
    mu = jnp.mean(c, axis=-1, keepdims=True)
    xc = c - mu
    ob_ref[...] = _silu(xc * lax.rsqrt(jnp.mean(xc * xc, axis=-1, keepdims=True) + LN_EPS) * lng_ref[...] + lnb_ref[...])

    lane = lax.broadcasted_iota(jnp.int32, (1, LANES), 1)
    lo64 = lane < 64
    first32 = (lane % 64) < 32
    cos = cos_ref[...]
    sin = sin_ref[...]

    def norm_rope(x, gw):
        x2 = x * x
        ssl = jnp.sum(jnp.where(lo64, x2, 0.0), axis=-1, keepdims=True)
        ssr = jnp.sum(jnp.where(lo64, 0.0, x2), axis=-1, keepdims=True)
        r = jnp.where(lo64, lax.rsqrt(ssl * (1.0 / HEAD_DIM) + RMS_EPS), lax.rsqrt(ssr * (1.0 / HEAD_DIM) + RMS_EPS))
        y = x * r * gw
        partner = jnp.where(first32, pltpu.roll(y, LANES - 32, 1), pltpu.roll(y, 32, 1))
        return y * cos + partner * sin

    for p in range(GROUP):
        qh_scr[:, p * LANES:(p + 1) * LANES] = norm_rope(qk_ref[:, p * LANES:(p + 1) * LANES], qn_ref[...])
    khat = norm_rope(qk_ref[:, 512:640], kn_ref[...])
    kh_ref[...] = khat
    qh_scr[:, 512:640] = khat

    z = _dot(ga_ref[...].astype(BF16), wg_ref[...]) + bg_ref[...]
    nk = GLA_HEADS * GLA_DK
    gl_scr[:, 0:nk] = jnp.exp(_log_sigmoid(z) * (1.0 / GLA_GATE_NORM))
    gl_scr[:, nk:2 * nk] = gqk_ref[:, 0:nk] * (GLA_DK ** -0.5)
    gl_scr[:, 2 * nk:3 * nk] = gqk_ref[:, nk:2 * nk]

    row_i = lax.broadcasted_iota(jnp.int32, (LANES, LANES), 0)
    eye = row_i == lax.broadcasted_iota(jnp.int32, (LANES, LANES), 1)
    top = row_i < GLA_DK
    key_bias = jnp.where(lax.broadcasted_iota(jnp.int32, (WINDOW, 1), 0) == 0, NEG_INF, 0.0)

    def to_col(row):
        return jnp.sum(jnp.where(eye, jnp.broadcast_to(row, (LANES, LANES)), 0.0), axis=-1, keepdims=True)

    for b in range(sb):
        kmat = kc_ref[b]
        vmat = vc_ref[b]
        k_new = qh_scr[b:b + 1, 512:640]
        v_new = v_ref[b:b + 1, :]
        for p in range(GROUP):
            q_row = qh_scr[b:b + 1, p * LANES:(p + 1) * LANES]
            prod = kmat * q_row
            pn = k_new * q_row
            outs = []
            for side in range(2):
                sel = lo64 == (side == 0)
                s = jnp.sum(jnp.where(sel, prod, 0.0), axis=-1, keepdims=True) * (HEAD_DIM ** -0.5) + key_bias
                s_n = jnp.sum(jnp.where(sel, pn, 0.0), axis=-1, keepdims=True) * (HEAD_DIM ** -0.5)
                sink = sink_ref[p + GROUP * side:p + GROUP * side + 1, 0:1]
                m = jnp.maximum(jnp.maximum(jnp.max(s, axis=0, keepdims=True), s_n), sink)
                e = jnp.exp(s - m)
                e_n = jnp.exp(s_n - m)
                denom = jnp.sum(e, axis=0, keepdims=True) + e_n + jnp.exp(sink - m)
                outs.append((jnp.sum(e * vmat, axis=0, keepdims=True) + e_n * v_new) / denom)
            oc_ref[b:b + 1, p * LANES:(p + 1) * LANES] = jnp.where(lo64, outs[0], outs[1])

        for p in range(GLA_HEADS // 2):
            lanes = slice(p * LANES, (p + 1) * LANES)
            a_col = to_col(gl_scr[b:b + 1, p * LANES:(p + 1) * LANES])
            q_col = to_col(gl_scr[b:b + 1, nk + p * LANES:nk + (p + 1) * LANES])
            k_col = to_col(gl_scr[b:b + 1, 2 * nk + p * LANES:2 * nk + (p + 1) * LANES])
            v0 = gv_ref[b:b + 1, 2 * p * GLA_DV:(2 * p + 1) * GLA_DV]
            v1 = gv_ref[b:b + 1, (2 * p + 1) * GLA_DV:(2 * p + 2) * GLA_DV]
            vsel = jnp.where(top, jnp.broadcast_to(v0, (LANES, GLA_DV)), jnp.broadcast_to(v1, (LANES, GLA_DV)))
            s_new = a_col * s_ref[b, lanes, :] + k_col * vsel
            sn_ref[b, lanes, :] = s_new
            w = q_col * s_new
            od_ref[b:b + 1, 2 * p * GLA_DV:(2 * p + 1) * GLA_DV] = jnp.sum(w[0:GLA_DK], axis=0, keepdims=True)
            od_ref[b:b + 1, (2 * p + 1) * GLA_DV:(2 * p + 2) * GLA_DV] = jnp.sum(w[GLA_DK:], axis=0, keepdims=True)

    gn = gn_ref[...]
    for h in range(GLA_HEADS):
        hs = slice(h * GLA_DV, (h + 1) * GLA_DV)
        od_ref[:, hs] = _rms(od_ref[:, hs], gn) * _silu(gg_ref[:, hs])


def _sample_mixers(ab, qk, v, gqk, gv, gg, ga, sct, cft, kc, vc, s0, small, sb):
    n = ab.shape[0]
    bw = BRANCH_WIDTH
    row = lambda i: (i, 0)
    lead3 = lambda i: (i, 0, 0)
    mid3 = lambda i: (0, i, 0)
    return pl.pallas_call(
        functools.partial(_sample_body, sb=sb),
        grid=(n // sb,),
        in_specs=[pl.BlockSpec((sb, W_AB), row), pl.BlockSpec((sb, 640), row), pl.BlockSpec((sb, LANES), row),
                  pl.BlockSpec((sb, 512), row),
                  pl.BlockSpec((sb, 512), row), pl.BlockSpec((sb, 512), row), pl.BlockSpec((sb, LANES), row),
                  pl.BlockSpec((SC_CONV_W - 1, sb, bw), mid3), pl.BlockSpec((CF_CONV_W - 1, sb, bw), mid3),
                  pl.BlockSpec((sb, WINDOW, LANES), lead3), pl.BlockSpec((sb, WINDOW, LANES), lead3),
                  pl.BlockSpec((sb, GLA_HEADS * GLA_DK, GLA_DV), lead3)]
                 + [_resident(a.shape) for a in small],
        out_specs=[pl.BlockSpec((sb, bw), row)] * 6 + [pl.BlockSpec((sb, LANES), row),
                                                       pl.BlockSpec((sb, GLA_HEADS * GLA_DK, GLA_DV), lead3)],
        out_shape=[jax.ShapeDtypeStruct((n, bw), F32)] * 6 + [jax.ShapeDtypeStruct((n, LANES), F32),
                                                              jax.ShapeDtypeStruct(s0.shape, F32)],
        scratch_shapes=[pltpu.VMEM((sb, 640), F32), pltpu.VMEM((sb, 3 * GLA_HEADS * GLA_DK), F32)],
        compiler_params=_params(1),
        name="sample_mixers",
    )(ab, qk, v, gqk, gv, gg, ga, sct, cft, kc, vc, s0, *small)


def _lane_dims(rope_split):
    lane = np.arange(LANES)
    if rope_split:
        return (lane // 64) * 32 + lane % 32
    return lane % HEAD_DIM


def _qk_weight(w_in, rope_split):
    half = HEAD_DIM // 2
    wq = w_in[:, OFF_Q:OFF_K].reshape(D_MODEL, N_KV_HEADS, GROUP, 2, half)
    wk = w_in[:, OFF_K:OFF_V].reshape(D_MODEL, N_KV_HEADS, 2, half)
    if rope_split:
        wq = wq.transpose(0, 2, 3, 1, 4)
        wk = wk.transpose(0, 2, 1, 3)
    else:
        wq = wq.transpose(0, 2, 1, 3, 4)
    return jnp.concatenate([wq.reshape(D_MODEL, N_HEADS * HEAD_DIM), wk.reshape(D_MODEL, N_KV_HEADS * HEAD_DIM)], axis=1)


def _rope_tables(pos, dims):
    half = HEAD_DIM // 2
    inv = ROPE_THETA ** (-jnp.arange(half, dtype=F32) / half)
    ang = pos.astype(F32)[:, None] * inv[None, :]
    cos = jnp.cos(ang)[:, dims % half]
    sin = jnp.sin(ang)[:, dims % half] * jnp.where(dims < half, -1.0, 1.0).astype(F32)[None, :]
    return cos, sin


def _rest_weight(w_in):
    pad = jnp.zeros((D_MODEL, LANES - GLA_GATE_RANK), w_in.dtype)
    return jnp.concatenate([w_in[:, OFF_V:OFF_GATES], pad], axis=1)


def kernel(x_prompt, x_sample, cache_swa_k, cache_swa_v, state_sconv, state_cconv, state_gla, norm_mix, w_in,
           conv_short, conv_conf, conv_conf_b, conf_ln_g, conf_ln_b, q_norm, k_norm, attn_sinks, w_gla_gate,
           b_gla_gate, gla_norm, w_branch, w_out, norm_ffn, w_ffn_in, w_ffn_out):
    batch, seq, _ = x_prompt.shape
    n_dec = x_sample.shape[0]
    depth = w_in.shape[0]
    bw = BRANCH_WIDTH

    dims_p, dims_s = _lane_dims(True), _lane_dims(False)
    cos_p, sin_p = _rope_tables(jnp.arange(seq), dims_p)
    cos_s, sin_s = _rope_tables(PAST_LEN + jnp.arange(1), dims_s)
    head_order = np.array([p + GROUP * side for p in range(GROUP) for side in range(2)])
    rest_widths = (LANES, 512, 512, 512, LANES)

    yp = x_prompt.reshape(batch * seq, D_MODEL)
    ys = x_sample.reshape(n_dec, D_MODEL)
    outs = {k: [] for k in ("sc_p", "sc_s", "cf_p", "cf_s", "k_p", "k_s", "v_p", "v_s", "S_p", "S_s")}
    row2 = lambda a: a.reshape(1, -1)

    for l in range(depth):
        wl = w_in[l]
        w_ab = wl[:, :W_AB].astype(BF16)
        w_qk_p = _qk_weight(wl, True).astype(BF16)
        w_qk_s = _qk_weight(wl, False).astype(BF16)
        w_rest = _rest_weight(wl).astype(BF16)
        w_gates = wl[:, OFF_GATES:].astype(BF16)
        wbr_c = w_branch[l, 2].reshape(N_KV_HEADS, GROUP, HEAD_DIM, D_MODEL).transpose(1, 0, 2, 3).reshape(bw, D_MODEL)
        wbr = jnp.stack([w_branch[l, 0], w_branch[l, 1], wbr_c, w_branch[l, 3]]).astype(BF16)
        wout = w_out[l].astype(BF16)
        wfi = w_ffn_in[l].astype(BF16)
        wfo = w_ffn_out[l].astype(BF16)
        wg = jnp.concatenate([w_gla_gate[l], jnp.zeros((LANES - GLA_GATE_RANK, GLA_HEADS * GLA_DK), F32)], axis=0).astype(BF16)
        bg = row2(b_gla_gate[l])
        gn = row2(gla_norm[l])
        gm = row2(norm_mix[l])
        gf = row2(norm_ffn[l])
        sinks_s = jnp.broadcast_to(attn_sinks[l][:, None], (N_HEADS, LANES))
        sinks_p = jnp.broadcast_to(attn_sinks[l][head_order][:, None, None], (N_HEADS, 1, LANES))
        cs, ccb = conv_short[l], row2(conv_conf_b[l])
        cc = jnp.broadcast_to(conv_conf[l][:, None, :], (CF_CONV_W, SUBLANES, bw))
        lng, lnb = row2(conf_ln_g[l]), row2(conf_ln_b[l])

        qk, v, gqk, gv, gg, ga = _inproj(yp, gm, (w_qk_p, w_rest), ((640,), rest_widths), 512)
        oc, kc, vc = _swa_prompt(qk, v, cos_p, sin_p, row2(q_norm[l][dims_p]), row2(k_norm[l][dims_p]), sinks_p, batch, seq, 4)
        od, st = _gla_prompt(gqk, gv, gg, ga, wg, bg, gn, batch, seq, 512)
        oa, ob, sc_new, cf_new = _convs_prompt(yp, gm, w_ab, cs, cc, ccb, lng, lnb, batch, seq, 512)
        yp = _merge_ffn(yp, oa, ob, oc, od, gm, w_gates, wbr, wout, gf, wfi, wfo, 512)
        outs["sc_p"].append(sc_new.reshape(batch, U_PAD, bw)[:, U_PAD - (SC_CONV_W - 1):])
        outs["cf_p"].append(cf_new.reshape(batch, G_PAD, bw)[:, G_PAD - (CF_CONV_W - 1):])
        outs["k_p"].append(kc.reshape(batch, WINDOW, 2, N_KV_HEADS, HEAD_DIM // 2).transpose(0, 1, 3, 2, 4)
                           .reshape(batch, WINDOW, N_KV_HEADS, HEAD_DIM))
        outs["v_p"].append(vc.reshape(batch, WINDOW, N_KV_HEADS, HEAD_DIM))
        st = st.reshape(batch, GLA_HEADS // 2, 2, GLA_DV, 2, GLA_DK)
        outs["S_p"].append(jnp.stack([st[:, :, 0, :, 0], st[:, :, 1, :, 1]], axis=2)
                           .reshape(batch, GLA_HEADS, GLA_DV, GLA_DK).transpose(0, 1, 3, 2))

        ab_s, qk_s, v_s, gqk_s, gv_s, gg_s, ga_s = _inproj(
            ys, gm, (w_ab, w_qk_s, w_rest), ((W_AB,), (640,), rest_widths), n_dec)
        small = (cs, cc, ccb, lng, lnb, row2(q_norm[l][dims_s]), row2(k_norm[l][dims_s]), cos_s, sin_s, sinks_s, wg, bg, gn)
        oa_s, ob_s, oc_s, od_s, u_s, g_s, kh_s, s_new = _sample_mixers(
            ab_s, qk_s, v_s, gqk_s, gv_s, gg_s, ga_s,
            state_sconv[l].transpose(1, 0, 2), state_cconv[l].transpose(1, 0, 2),
            cache_swa_k[l].reshape(n_dec, WINDOW, LANES), cache_swa_v[l].reshape(n_dec, WINDOW, LANES),
            state_gla[l].reshape(n_dec, GLA_HEADS * GLA_DK, GLA_DV), small, 8)
        ys = _merge_ffn(ys, oa_s, ob_s, oc_s, od_s, gm, w_gates, wbr, wout, gf, wfi, wfo, n_dec)
        outs["sc_s"].append(jnp.concatenate([state_sconv[l][:, 1:], u_s[:, None]], axis=1))
        outs["cf_s"].append(jnp.concatenate([state_cconv[l][:, 1:], g_s[:, None]], axis=1))
        outs["k_s"].append(jnp.concatenate(
            [cache_swa_k[l][:, 1:], kh_s.reshape(n_dec, 1, N_KV_HEADS, HEAD_DIM)], axis=1))
        outs["v_s"].append(jnp.concatenate(
            [cache_swa_v[l][:, 1:], v_s.reshape(n_dec, 1, N_KV_HEADS, HEAD_DIM)], axis=1))
        outs["S_s"].append(s_new.reshape(n_dec, GLA_HEADS, GLA_DK, GLA_DV))

    return (yp.reshape(batch, seq, D_MODEL), ys.reshape(n_dec, 1, D_MODEL),
            jnp.stack(outs["sc_p"]), jnp.stack(outs["sc_s"]), jnp.stack(outs["cf_p"]), jnp.stack(outs["cf_s"]),
            jnp.stack(outs["k_p"]), jnp.stack(outs["k_s"]), jnp.stack(outs["v_p"]), jnp.stack(outs["v_s"]),
            jnp.stack(outs["S_p"]), jnp.stack(outs["S_s"]))
```

```python
import functools

import numpy as np
import jax
import jax.numpy as jnp
from jax import lax
from jax.experimental import pallas as pl
from jax.experimental.pallas import tpu as pltpu

F32 = jnp.float32
BF16 = jnp.bfloat16

D_MODEL = 1024
PAST_LEN = 16384
N_BRANCH = 4
BRANCH_WIDTH = D_MODEL // 2
SC_CONV_W = 3
CF_CONV_W = 31
N_HEADS = 8
N_KV_HEADS = 2
HEAD_DIM = 64
GROUP = N_HEADS // N_KV_HEADS
WINDOW = 128
ATTN_BLOCK = 128
ROPE_THETA = 10000.0
GLA_HEADS = 4
GLA_DK = 64
GLA_DV = BRANCH_WIDTH // GLA_HEADS
GLA_GATE_RANK = 16
GLA_GATE_NORM = 16.0
GLA_CHUNK = 64
D_FF = -(-8 * D_MODEL // (3 * 256)) * 256
NEG_INF = -1e30
RMS_EPS = 1e-6
LN_EPS = 1e-5

LANES = 128
SUBLANES = 8
FF_CHUNK = 256
VMEM_LIMIT = 56 * 1024 * 1024

OFF_AB = 0
W_AB = 5 * BRANCH_WIDTH
OFF_Q = W_AB
OFF_K = OFF_Q + N_HEADS * HEAD_DIM
OFF_V = OFF_K + N_KV_HEADS * HEAD_DIM
OFF_GQ = OFF_V + N_KV_HEADS * HEAD_DIM
OFF_GK = OFF_GQ + GLA_HEADS * GLA_DK
OFF_GV = OFF_GK + GLA_HEADS * GLA_DK
OFF_GG = OFF_GV + GLA_HEADS * GLA_DV
OFF_GA = OFF_GG + GLA_HEADS * GLA_DV
OFF_GATES = OFF_GA + GLA_GATE_RANK


class _Layer:
    def __init__(self, stacked, layer):
        self.stacked, self.layer = stacked, layer


def _consts(params):
    specs, operands = [], []
    for p in params:
        if isinstance(p, _Layer):
            shape, layer = p.stacked.shape[1:], p.layer
            specs.append(pl.BlockSpec((None,) + shape, lambda *_, layer=layer, nd=len(shape): (layer,) + (0,) * nd,
                                      pipeline_mode=pl.Buffered(1)))
            operands.append(p.stacked)
        else:
            specs.append(pl.BlockSpec(p.shape, lambda *_, nd=p.ndim: (0,) * nd, pipeline_mode=pl.Buffered(1)))
            operands.append(p)
    return specs, operands


def _params(n_axes):
    return pltpu.CompilerParams(dimension_semantics=("arbitrary",) * n_axes, vmem_limit_bytes=VMEM_LIMIT)


def _rms(x, g):
    return x * lax.rsqrt(jnp.mean(x * x, axis=-1, keepdims=True) + RMS_EPS) * g


def _dot(a, b):
    return jnp.dot(a, b, preferred_element_type=F32)


def _dot_nt(a, b):
    return lax.dot_general(a, b, (((1,), (1,)), ((), ())), preferred_element_type=F32)


def _dot_tn(a, b):
    return lax.dot_general(a, b, (((0,), (0,)), ((), ())), preferred_element_type=F32)


def _log_sigmoid(z):
    return -(jnp.maximum(-z, 0.0) + jnp.log1p(jnp.exp(-jnp.abs(z))))


def _silu(x):
    return x * jax.nn.sigmoid(x)


def _inproj_body(x_ref, g_ref, *refs, widths):
    w_refs, out_refs = refs[:len(widths)], list(refs[len(widths):])
    xn = _rms(x_ref[...], g_ref[...]).astype(BF16)
    for w_ref, group in zip(w_refs, widths):
        off = 0
        for wd in group:
            out_refs.pop(0)[...] = _dot(xn, w_ref[:, off:off + wd])
            off += wd


def _inproj(x2d, g, weights, widths, tm):
    n = x2d.shape[0]
    flat = [wd for group in widths for wd in group]
    const_specs, const_ops = _consts((g,) + tuple(weights))
    return pl.pallas_call(
        functools.partial(_inproj_body, widths=widths),
        grid=(n // tm,),
        in_specs=[pl.BlockSpec((tm, D_MODEL), lambda i: (i, 0))] + const_specs,
        out_specs=[pl.BlockSpec((tm, wd), lambda i: (i, 0)) for wd in flat],
        out_shape=[jax.ShapeDtypeStruct((n, wd), F32) for wd in flat],
        compiler_params=_params(1),
        name="inproj",
    )(x2d, *const_ops)


def _swa_body(qk_ref, v_ref, cos_ref, sin_ref, qn_ref, kn_ref, sink_ref, oc_ref, kc_ref, vc_ref, kprev, vprev, *, nblk):
    j = pl.program_id(1)
    tq = ATTN_BLOCK

    @pl.when(j == 0)
    def _():
        kprev[...] = jnp.zeros_like(kprev)
        vprev[...] = jnp.zeros_like(vprev)

    blk = qk_ref[...]
    cos = cos_ref[...]
    sin = sin_ref[...]
    lane = lax.broadcasted_iota(jnp.int32, (1, LANES), 1)
    left = (lane // 32) % 2 == 0
    lo64 = lane < 64

    def norm_rope(x, g):
        x2 = x * x
        ssl = jnp.sum(jnp.where(left, x2, 0.0), axis=-1, keepdims=True)
        ssr = jnp.sum(jnp.where(left, 0.0, x2), axis=-1, keepdims=True)
        r = jnp.where(left, lax.rsqrt(ssl * (1.0 / HEAD_DIM) + RMS_EPS), lax.rsqrt(ssr * (1.0 / HEAD_DIM) + RMS_EPS))
        y = x * r * g
        return y * cos + pltpu.roll(y, 64, 1) * sin

    khat = norm_rope(blk[:, 512:640], kn_ref[...])
    v = v_ref[...]
    k_all = jnp.concatenate([kprev[...], khat], axis=0).astype(BF16)
    v_all = jnp.concatenate([vprev[...], v], axis=0).astype(BF16)
    qhat = [norm_rope(blk[:, p * LANES:(p + 1) * LANES], qn_ref[...]) * (HEAD_DIM ** -0.5) for p in range(GROUP)]
    upper = lax.broadcasted_iota(jnp.int32, (tq, tq), 1) > lax.broadcasted_iota(jnp.int32, (tq, tq), 0)
    sink = sink_ref[...][:, :, 0:1]

    for i in range(nblk):
        rows = slice(i * tq, (i + 1) * tq)
        qs = []
        for p in range(GROUP):
            qs.append(jnp.where(left, qhat[p][rows], 0.0))
            qs.append(jnp.where(left, 0.0, qhat[p][rows]))
        q_all = jnp.concatenate(qs, axis=0).astype(BF16)
        s2 = _dot_nt(q_all, k_all[i * tq:(i + 2) * tq]).reshape(N_HEADS, tq, 2 * tq)
        s_prev = s2[:, :, :tq]
        if i == 0:
            s_prev = s_prev + jnp.where(j > 0, 0.0, NEG_INF)
        s = jnp.where(upper, s_prev, s2[:, :, tq:])
        m = jnp.maximum(jnp.max(s, axis=-1, keepdims=True), sink)
        e = jnp.exp(s - m)
        inv = 1.0 / (jnp.sum(e, axis=-1, keepdims=True) + jnp.exp(sink - m))
        e2 = jnp.concatenate([jnp.where(upper, e, 0.0), jnp.where(upper, 0.0, e)], axis=-1).astype(BF16)
        o = _dot(e2.reshape(N_HEADS * tq, 2 * tq), v_all[i * tq:(i + 2) * tq]).reshape(N_HEADS, tq, LANES) * inv
        for p in range(GROUP):
            oc_ref[rows, p * LANES:(p + 1) * LANES] = jnp.where(lo64, o[2 * p], o[2 * p + 1]).astype(oc_ref.dtype)

    last = slice((nblk - 1) * tq, nblk * tq)
    kprev[...] = khat[last]
    vprev[...] = v[last]
    kc_ref[...] = khat[last]
    vc_ref[...] = v[last]


def _swa_prompt(qk, v, cos, sin, qn, kn, sinks, batch, seq, nblk):
    ts = nblk * ATTN_BLOCK
    nb = seq // ts
    row = lambda b, j: (b * nb + j, 0)
    const_specs, const_ops = _consts((qn, kn, sinks))
    return pl.pallas_call(
        functools.partial(_swa_body, nblk=nblk),
        grid=(batch, nb),
        in_specs=[pl.BlockSpec((ts, 640), row), pl.BlockSpec((ts, LANES), row),
                  pl.BlockSpec((ts, LANES), lambda b, j: (j, 0)),
                  pl.BlockSpec((ts, LANES), lambda b, j: (j, 0))] + const_specs,
        out_specs=[pl.BlockSpec((ts, 512), row),
                   pl.BlockSpec((ATTN_BLOCK, LANES), lambda b, j: (b, 0)),
                   pl.BlockSpec((ATTN_BLOCK, LANES), lambda b, j: (b, 0))],
        out_shape=[jax.ShapeDtypeStruct((batch * seq, 512), BF16),
                   jax.ShapeDtypeStruct((batch * WINDOW, LANES), F32),
                   jax.ShapeDtypeStruct((batch * WINDOW, LANES), F32)],
        scratch_shapes=[pltpu.VMEM((ATTN_BLOCK, LANES), F32), pltpu.VMEM((ATTN_BLOCK, LANES), F32)],
        compiler_params=_params(2),
        name="swa_prompt",
    )(qk, v, cos, sin, *const_ops)


def _split3(x):
    hi = x.astype(BF16)
    r1 = x - hi.astype(F32)
    mid = r1.astype(BF16)
    lo = (r1 - mid.astype(F32)).astype(BF16)
    return hi, mid, lo


def _gla_body(gqk_ref, gv_ref, gg_ref, ga_ref, wg_ref, bg_ref, gn_ref, od_ref, st_ref, st_scr, *, tg):
    t = pl.program_id(1)
    ck = GLA_CHUNK

    @pl.when(t == 0)
    def _():
        st_scr[...] = jnp.zeros_like(st_scr)

    z = _dot(ga_ref[...].astype(BF16), wg_ref[...]) + bg_ref[...]
    la = _log_sigmoid(z) * (1.0 / GLA_GATE_NORM)
    r_i = lax.broadcasted_iota(jnp.int32, (tg, tg), 0)
    c_i = lax.broadcasted_iota(jnp.int32, (tg, tg), 1)
    same = (r_i // ck) == (c_i // ck)
    tri = jnp.where(same & (c_i <= r_i), 1.0, 0.0).astype(BF16)
    ones = jnp.where(same, 1.0, 0.0).astype(BF16)
    parts = _split3(la)
    bcum = _dot(tri, parts[0]) + _dot(tri, parts[1]) + _dot(tri, parts[2])
    btot = _dot(ones, parts[0]) + _dot(ones, parts[1]) + _dot(ones, parts[2])

    gqk = gqk_ref[...]
    nk = GLA_HEADS * GLA_DK
    gk = gqk[:, nk:]
    qe = (gqk[:, :nk] * (GLA_DK ** -0.5)) * jnp.exp(bcum)
    ke = gk * jnp.exp(-bcum)
    kd = gk * jnp.exp(btot - bcum)
    dec = jnp.exp(btot)

    lane = lax.broadcasted_iota(jnp.int32, (1, LANES), 1)
    lo64 = lane < 64
    causal = lax.broadcasted_iota(jnp.int32, (ck, ck), 1) <= lax.broadcasted_iota(jnp.int32, (ck, ck), 0)
    u_row = lax.broadcasted_iota(jnp.int32, (2 * GLA_DV, LANES), 0)
    u_lane = lax.broadcasted_iota(jnp.int32, (2 * GLA_DV, LANES), 1)
    diag_blocks = (u_row < GLA_DV) == (u_lane < GLA_DK)
    gn = gn_ref[...]

    for c in range(tg // ck):
        rows = slice(c * ck, (c + 1) * ck)
        for p in range(GLA_HEADS // 2):
            lanes = slice(p * LANES, (p + 1) * LANES)
            vcols = slice(p * 2 * GLA_DV, (p + 1) * 2 * GLA_DV)
            qe_p = qe[rows, lanes]
            ke_p = ke[rows, lanes].astype(BF16)
            kd_p = kd[rows, lanes].astype(BF16)
            v_p = gv_ref[rows, vcols].astype(BF16)
            st = st_scr[p]
            o_halves = []
            for side in range(2):
                qm = jnp.where(lo64 == (side == 0), qe_p, 0.0).astype(BF16)
                a = jnp.where(causal, _dot_nt(qm, ke_p), 0.0).astype(BF16)
                o_halves.append(_dot(a, v_p[:, side * GLA_DV:(side + 1) * GLA_DV]))
            o = jnp.concatenate(o_halves, axis=1) + _dot_nt(qe_p.astype(BF16), st.astype(BF16))
            upd = jnp.where(diag_blocks, _dot_tn(v_p, kd_p), 0.0)
            st_scr[p] = dec[c * ck:c * ck + 1, lanes] * st + upd
            gate = gg_ref[rows, vcols]
            for side in range(2):
                hs = slice(side * GLA_DV, (side + 1) * GLA_DV)
                od_ref[rows, p * 2 * GLA_DV + side * GLA_DV:p * 2 * GLA_DV + (side + 1) * GLA_DV] = (
                    _rms(o[:, hs], gn) * _silu(gate[:, hs])).astype(od_ref.dtype)

    st_ref[...] = st_scr[...]


def _gla_prompt(gqk, gv, gg, ga, wg, bg, gn, batch, seq, tg):
    nt = seq // tg
    row = lambda b, t: (b * nt + t, 0)
    const_specs, const_ops = _consts((wg, bg, gn))
    return pl.pallas_call(
        functools.partial(_gla_body, tg=tg),
        grid=(batch, nt),
        in_specs=[pl.BlockSpec((tg, 512), row), pl.BlockSpec((tg, 512), row), pl.BlockSpec((tg, 512), row),
                  pl.BlockSpec((tg, LANES), row)] + const_specs,
        out_specs=[pl.BlockSpec((tg, 512), row),
                   pl.BlockSpec((None, 2, 2 * GLA_DV, LANES), lambda b, t: (b, 0, 0, 0))],
        out_shape=[jax.ShapeDtypeStruct((batch * seq, 512), BF16),
                   jax.ShapeDtypeStruct((batch, 2, 2 * GLA_DV, LANES), F32)],
        scratch_shapes=[pltpu.VMEM((2, 2 * GLA_DV, LANES), F32)],
        compiler_params=_params(2),
        name="gla_prompt",
    )(gqk, gv, gg, ga, *const_ops)


U_PAD = 8
G_PAD = 32
CONV_ROWS = 16


def _conv_prepare(xn, w_ref, cs_ref, ubuf, gbuf, gsh, oa_scr, tm):
    i = pl.program_id(1)
    bw = BRANCH_WIDTH

    @pl.when(i == 0)
    def _():
        ubuf[0:U_PAD, :] = jnp.zeros((U_PAD, bw), F32)
        gbuf[0:G_PAD, :] = jnp.zeros((G_PAD, bw), F32)

    @pl.when(i > 0)
    def _():
        ubuf[0:U_PAD, :] = ubuf[tm:tm + U_PAD, :]
        gbuf[0:G_PAD, :] = gbuf[tm:tm + G_PAD, :]

    sc_h = _dot(xn, w_ref[:, 0:bw])
    sc_c = _dot(xn, w_ref[:, 2 * bw:3 * bw])
    u = sc_c * sc_h
    ubuf[U_PAD:U_PAD + tm, :] = u
    conv_u = (cs_ref[0:1, :] * ubuf[U_PAD - 2:U_PAD - 2 + tm, :] + cs_ref[1:2, :] * ubuf[U_PAD - 1:U_PAD - 1 + tm, :]
              + cs_ref[2:3, :] * u)
    oa_scr[...] = (_dot(xn, w_ref[:, bw:2 * bw]) * conv_u).astype(oa_scr.dtype)

    cf_a = _dot(xn, w_ref[:, 3 * bw:4 * bw])
    cf_b = _dot(xn, w_ref[:, 4 * bw:5 * bw])
    gbuf[G_PAD:G_PAD + tm, :] = cf_a * jax.nn.sigmoid(cf_b)
    span = tm + G_PAD - SUBLANES
    for sh in range(1, SUBLANES):
        gsh[sh - 1] = gbuf[sh:sh + span, :]


def _conv_block(r0, cc_ref, gbuf, gsh, cbuf):
    base = G_PAD - (CF_CONV_W - 1)
    acc = None
    for k in range(CF_CONV_W):
        al, sh = divmod(base + k, SUBLANES)
        rows = pl.ds(pl.multiple_of(r0 + al * SUBLANES, SUBLANES), CONV_ROWS)
        w_tap = jnp.concatenate([cc_ref[k]] * (CONV_ROWS // SUBLANES), axis=0)
        tap = w_tap * (gbuf[rows, :] if sh == 0 else gsh[sh - 1, rows, :])
        acc = tap if acc is None else acc + tap
    cbuf[pl.ds(pl.multiple_of(r0, CONV_ROWS), CONV_ROWS), :] = acc


def _conv_finish(cbuf, ccb_ref, lng_ref, lnb_ref):
    c = cbuf[...] + ccb_ref[...]
    mu = jnp.mean(c, axis=-1, keepdims=True)
    xc = c - mu
    y = xc * lax.rsqrt(jnp.mean(xc * xc, axis=-1, keepdims=True) + LN_EPS) * lng_ref[...] + lnb_ref[...]
    return _silu(y)


MERGE_COLS = 256
MERGE_CHUNKS = D_MODEL // MERGE_COLS


def _merge_term(xn, o, r, c, wgt_ref, wbr_ref):
    idx = r * MERGE_CHUNKS + c
    return jax.nn.sigmoid(_dot(xn, wgt_ref[idx])) * _dot(o, wbr_ref[idx])


def _merge_out(x, xn, branches, wgt_ref, wbr_ref, wout_ref):
    y = x
    for c in range(MERGE_CHUNKS):
        merged = None
        for r, o in enumerate(branches):
            term = _merge_term(xn, o.astype(BF16), r, c, wgt_ref, wbr_ref)
            merged = term if merged is None else merged + term
        y = y + _dot(merged.astype(BF16), wout_ref[c * MERGE_COLS:(c + 1) * MERGE_COLS, :])
    return y


def _swiglu(x1, gf_ref, wfi_ref, wfo_ref):
    xn2 = _rms(x1, gf_ref[...]).astype(BF16)
    y = x1
    for c0 in range(0, D_FF, FF_CHUNK):
        h_gate = _dot(xn2, wfi_ref[:, c0:c0 + FF_CHUNK])
        h_up = _dot(xn2, wfi_ref[:, D_FF + c0:D_FF + c0 + FF_CHUNK])
        y = y + _dot((_silu(h_gate) * h_up).astype(BF16), wfo_ref[c0:c0 + FF_CHUNK, :])
    return y


def _mix_body(x_ref, oc_ref, od_ref, gm_ref, wab_ref, cs_ref, cc_ref, ccb_ref, lng_ref, lnb_ref,
              wgt_ref, wbr_ref, wout_ref, x1_ref, sc_ref, cf_ref, ubuf, gbuf, gsh, cbuf, xn_scr, oa_scr, macc, *, tm):
    xn_scr[...] = _rms(x_ref[...], gm_ref[...]).astype(BF16)
    _conv_prepare(xn_scr[...], wab_ref, cs_ref, ubuf, gbuf, gsh, oa_scr, tm)

    rows_per_step = tm // MERGE_CHUNKS

    def step(c, carry):
        xn = xn_scr[...]
        macc[c] = (_merge_term(xn, oc_ref[...], 2, c, wgt_ref, wbr_ref) + _merge_term(xn, od_ref[...], 3, c, wgt_ref, wbr_ref)
                   + _merge_term(xn, oa_scr[...], 0, c, wgt_ref, wbr_ref))
        row0 = c * rows_per_step
        for r0 in range(0, rows_per_step, CONV_ROWS):
            _conv_block(row0 + r0, cc_ref, gbuf, gsh, cbuf)
        return carry

    lax.fori_loop(0, MERGE_CHUNKS, step, 0)

    o_b = _conv_finish(cbuf, ccb_ref, lng_ref, lnb_ref).astype(BF16)
    xn = xn_scr[...]
    y = x_ref[...]
    for c in range(MERGE_CHUNKS):
        merged = macc[c] + _merge_term(xn, o_b, 1, c, wgt_ref, wbr_ref)
        y = y + _dot(merged.astype(BF16), wout_ref[c * MERGE_COLS:(c + 1) * MERGE_COLS, :])
    x1_ref[...] = y
    sc_ref[...] = ubuf[tm:tm + U_PAD, :]
    cf_ref[...] = gbuf[tm:tm + G_PAD, :]


def _mix_prompt(x2d, oc, od, gm, w_ab, cs, cc, ccb, lng, lnb, wgt, wbr, wout, batch, seq, tm):
    nt = seq // tm
    row = lambda b, i: (b * nt + i, 0)
    bw = BRANCH_WIDTH
    const_specs, const_ops = _consts((gm, w_ab, cs, cc, ccb, lng, lnb, wgt, wbr, wout))
    return pl.pallas_call(
        functools.partial(_mix_body, tm=tm),
        grid=(batch, nt),
        in_specs=[pl.BlockSpec((tm, D_MODEL), row), pl.BlockSpec((tm, bw), row), pl.BlockSpec((tm, bw), row)]
                 + const_specs,
        out_specs=[pl.BlockSpec((tm, D_MODEL), row),
                   pl.BlockSpec((U_PAD, bw), lambda b, i: (b, 0)), pl.BlockSpec((G_PAD, bw), lambda b, i: (b, 0))],
        out_shape=[jax.ShapeDtypeStruct((batch * seq, D_MODEL), F32),
                   jax.ShapeDtypeStruct((batch * U_PAD, bw), F32), jax.ShapeDtypeStruct((batch * G_PAD, bw), F32)],
        scratch_shapes=[pltpu.VMEM((U_PAD + tm, bw), F32), pltpu.VMEM((G_PAD + tm, bw), F32),
                        pltpu.VMEM((SUBLANES - 1, tm + G_PAD - SUBLANES, bw), F32), pltpu.VMEM((tm, bw), F32),
                        pltpu.VMEM((tm, D_MODEL), BF16), pltpu.VMEM((tm, bw), BF16),
                        pltpu.VMEM((MERGE_CHUNKS, tm, MERGE_COLS), F32)],
        compiler_params=_params(2),
        name="mix_prompt",
    )(x2d, oc, od, *const_ops)


def _ffn_body(x1_ref, gf_ref, wfi_ref, wfo_ref, y_ref):
    y_ref[...] = _swiglu(x1_ref[...], gf_ref, wfi_ref, wfo_ref)


def _ffn(x2d, gf, wfi, wfo, tm):
    n = x2d.shape[0]
    row = lambda i: (i, 0)
    const_specs, const_ops = _consts((gf, wfi, wfo))
    return pl.pallas_call(
        _ffn_body,
        grid=(n // tm,),
        in_specs=[pl.BlockSpec((tm, D_MODEL), row)] + const_specs,
        out_specs=pl.BlockSpec((tm, D_MODEL), row),
        out_shape=jax.ShapeDtypeStruct((n, D_MODEL), F32),
        compiler_params=_params(1),
        name="ffn",
    )(x2d, *const_ops)


def _merge_body(x_ref, oa_ref, ob_ref, oc_ref, od_ref, gm_ref, wgt_ref, wbr_ref, wout_ref, x1_ref):
    x = x_ref[...]
    xn = _rms(x, gm_ref[...]).astype(BF16)
    x1_ref[...] = _merge_out(x, xn, (oa_ref[...], ob_ref[...], oc_ref[...], od_ref[...]), wgt_ref, wbr_ref, wout_ref)


def _merge(x2d, oa, ob, oc, od, gm, wgt, wbr, wout, tm):
    n = x2d.shape[0]
    bw = BRANCH_WIDTH
    row = lambda i: (i, 0)
    const_specs, const_ops = _consts((gm, wgt, wbr, wout))
    return pl.pallas_call(
        _merge_body,
        grid=(n // tm,),
        in_specs=[pl.BlockSpec((tm, D_MODEL), row)] + [pl.BlockSpec((tm, bw), row)] * 4 + const_specs,
        out_specs=pl.BlockSpec((tm, D_MODEL), row),
        out_shape=jax.ShapeDtypeStruct((n, D_MODEL), F32),
        compiler_params=_params(1),
        name="merge",
    )(x2d, oa, ob, oc, od, *const_ops)


def _sample_body(ab_ref, qk_ref, v_ref, gqk_ref, gv_ref, gg_ref, ga_ref, sct_ref, cft_ref, kc_ref, vc_ref, s_ref,
                 cs_ref, cc_ref, ccb_ref, lng_ref, lnb_ref, qn_ref, kn_ref, cos_ref, sin_ref, sink_ref,
                 wg_ref, bg_ref, gn_ref,
                 oa_ref, ob_ref, oc_ref, od_ref, u_ref, g_ref, kh_ref, sn_ref,
                 qh_scr, gl_scr, *, sb):
    bw = BRANCH_WIDTH
    sc_h = ab_ref[:, 0:bw]
    sc_b = ab_ref[:, bw:2 * bw]
    sc_c = ab_ref[:, 2 * bw:3 * bw]
    u = sc_c * sc_h
    u_ref[...] = u
    oa_ref[...] = sc_b * (cs_ref[0:1, :] * sct_ref[0] + cs_ref[1:2, :] * sct_ref[1] + cs_ref[2:3, :] * u)
    g = ab_ref[:, 3 * bw:4 * bw] * jax.nn.sigmoid(ab_ref[:, 4 * bw:5 * bw])
    g_ref[...] = g
    c = ccb_ref[...] + cc_ref[CF_CONV_W - 1, 0:1, :] * g
    for k in range(CF_CONV_W - 1):
        c = c + cc_ref[k, 0:1, :] * cft_ref[k]
    mu = jnp.mean(c, axis=-1, keepdims=True)
    xc = c - mu
    ob_ref[...] = _silu(xc * lax.rsqrt(jnp.mean(xc * xc, axis=-1, keepdims=True) + LN_EPS) * lng_ref[...] + lnb_ref[...])

    lane = lax.broadcasted_iota(jnp.int32, (1, LANES), 1)
    lo64 = lane < 64
    first32 = (lane % 64) < 32
    cos = cos_ref[...]
    sin = sin_ref[...]

    def norm_rope(x, gw):
        x2 = x * x
        ssl = jnp.sum(jnp.where(lo64, x2, 0.0), axis=-1, keepdims=True)
        ssr = jnp.sum(jnp.where(lo64, 0.0, x2), axis=-1, keepdims=True)
        r = jnp.where(lo64, lax.rsqrt(ssl * (1.0 / HEAD_DIM) + RMS_EPS), lax.rsqrt(ssr * (1.0 / HEAD_DIM) + RMS_EPS))
        y = x * r * gw
        partner = jnp.where(first32, pltpu.roll(y, LANES - 32, 1), pltpu.roll(y, 32, 1))
        return y * cos + partner * sin

    for p in range(GROUP):
        qh_scr[:, p * LANES:(p + 1) * LANES] = norm_rope(qk_ref[:, p * LANES:(p + 1) * LANES], qn_ref[...])
    khat = norm_rope(qk_ref[:, 512:640], kn_ref[...])
    kh_ref[...] = khat
    qh_scr[:, 512:640] = khat

    z = _dot(ga_ref[...].astype(BF16), wg_ref[...]) + bg_ref[...]
    nk = GLA_HEADS * GLA_DK
    gl_scr[:, 0:nk] = jnp.exp(_log_sigmoid(z) * (1.0 / GLA_GATE_NORM))
    gl_scr[:, nk:2 * nk] = gqk_ref[:, 0:nk] * (GLA_DK ** -0.5)
    gl_scr[:, 2 * nk:3 * nk] = gqk_ref[:, nk:2 * nk]

    row_i = lax.broadcasted_iota(jnp.int32, (LANES, LANES), 0)
    eye = row_i == lax.broadcasted_iota(jnp.int32, (LANES, LANES), 1)
    top = row_i < GLA_DK
    key_bias = jnp.where(lax.broadcasted_iota(jnp.int32, (WINDOW, 1), 0) == 0, NEG_INF, 0.0)

    def to_col(row):
        return jnp.sum(jnp.where(eye, jnp.broadcast_to(row, (LANES, LANES)), 0.0), axis=-1, keepdims=True)

    for b in range(sb):
        kmat = kc_ref[b]
        vmat = vc_ref[b]
        k_new = qh_scr[b:b + 1, 512:640]
        v_new = v_ref[b:b + 1, :]
        for p in range(GROUP):
            q_row = qh_scr[b:b + 1, p * LANES:(p + 1) * LANES]
            prod = kmat * q_row
            pn = k_new * q_row
            outs = []
            for side in range(2):
                sel = lo64 == (side == 0)
                s = jnp.sum(jnp.where(sel, prod, 0.0), axis=-1, keepdims=True) * (HEAD_DIM ** -0.5) + key_bias
                s_n = jnp.sum(jnp.where(sel, pn, 0.0), axis=-1, keepdims=True) * (HEAD_DIM ** -0.5)
                sink = sink_ref[p + GROUP * side:p + GROUP * side + 1, 0:1]
                m = jnp.maximum(jnp.maximum(jnp.max(s, axis=0, keepdims=True), s_n), sink)
                e = jnp.exp(s - m)
                e_n = jnp.exp(s_n - m)
                denom = jnp.sum(e, axis=0, keepdims=True) + e_n + jnp.exp(sink - m)
                outs.append((jnp.sum(e * vmat, axis=0, keepdims=True) + e_n * v_new) / denom)
            oc_ref[b:b + 1, p * LANES:(p + 1) * LANES] = jnp.where(lo64, outs[0], outs[1])

        for p in range(GLA_HEADS // 2):
            lanes = slice(p * LANES, (p + 1) * LANES)
            a_col = to_col(gl_scr[b:b + 1, p * LANES:(p + 1) * LANES])
            q_col = to_col(gl_scr[b:b + 1, nk + p * LANES:nk + (p + 1) * LANES])
            k_col = to_col(gl_scr[b:b + 1, 2 * nk + p * LANES:2 * nk + (p + 1) * LANES])
            v0 = gv_ref[b:b + 1, 2 * p * GLA_DV:(2 * p + 1) * GLA_DV]
            v1 = gv_ref[b:b + 1, (2 * p + 1) * GLA_DV:(2 * p + 2) * GLA_DV]
            vsel = jnp.where(top, jnp.broadcast_to(v0, (LANES, GLA_DV)), jnp.broadcast_to(v1, (LANES, GLA_DV)))
            s_new = a_col * s_ref[b, lanes, :] + k_col * vsel
            sn_ref[b, lanes, :] = s_new
            w = q_col * s_new
            od_ref[b:b + 1, 2 * p * GLA_DV:(2 * p + 1) * GLA_DV] = jnp.sum(w[0:GLA_DK], axis=0, keepdims=True)
            od_ref[b:b + 1, (2 * p + 1) * GLA_DV:(2 * p + 2) * GLA_DV] = jnp.sum(w[GLA_DK:], axis=0, keepdims=True)

    gn = gn_ref[...]
    for h in range(GLA_HEADS):
        hs = slice(h * GLA_DV, (h + 1) * GLA_DV)
        od_ref[:, hs] = _rms(od_ref[:, hs], gn) * _silu(gg_ref[:, hs])


def _sample_mixers(ab, qk, v, gqk, gv, gg, ga, sct, cft, kc, vc, s0, small, sb, layer):
    n = ab.shape[0]
    bw = BRANCH_WIDTH
    row = lambda i: (i, 0)
    lead3 = lambda i: (i, 0, 0)
    state_lead = lambda i: (layer, i, 0, 0)
    state_mid = lambda i: (layer, 0, i, 0)
    const_specs, const_ops = _consts(small)
    return pl.pallas_call(
        functools.partial(_sample_body, sb=sb),
        grid=(n // sb,),
        in_specs=[pl.BlockSpec((sb, W_AB), row), pl.BlockSpec((sb, 640), row), pl.BlockSpec((sb, LANES), row),
                  pl.BlockSpec((sb, 512), row),
                  pl.BlockSpec((sb, 512), row), pl.BlockSpec((sb, 512), row), pl.BlockSpec((sb, LANES), row),
                  pl.BlockSpec((None, SC_CONV_W - 1, sb, bw), state_mid),
                  pl.BlockSpec((None, CF_CONV_W - 1, sb, bw), state_mid),
                  pl.BlockSpec((None, sb, WINDOW, LANES), state_lead), pl.BlockSpec((None, sb, WINDOW, LANES), state_lead),
                  pl.BlockSpec((None, sb, GLA_HEADS * GLA_DK, GLA_DV), state_lead)] + const_specs,
        out_specs=[pl.BlockSpec((sb, bw), row)] * 6 + [pl.BlockSpec((sb, LANES), row),
                                                       pl.BlockSpec((sb, GLA_HEADS * GLA_DK, GLA_DV), lead3)],
        out_shape=[jax.ShapeDtypeStruct((n, bw), F32)] * 6 + [jax.ShapeDtypeStruct((n, LANES), F32),
                                                              jax.ShapeDtypeStruct(s0.shape[1:], F32)],
        scratch_shapes=[pltpu.VMEM((sb, 640), F32), pltpu.VMEM((sb, 3 * GLA_HEADS * GLA_DK), F32)],
        compiler_params=_params(1),
        name="sample_mixers",
    )(ab, qk, v, gqk, gv, gg, ga, sct, cft, kc, vc, s0, *const_ops)


def _lane_dims(rope_split):
    lane = np.arange(LANES)
    if rope_split:
        return (lane // 64) * 32 + lane % 32
    return lane % HEAD_DIM


def _qk_weight(w_in, rope_split):
    depth, half = w_in.shape[0], HEAD_DIM // 2
    wq = w_in[:, :, OFF_Q:OFF_K].reshape(depth, D_MODEL, N_KV_HEADS, GROUP, 2, half)
    wk = w_in[:, :, OFF_K:OFF_V].reshape(depth, D_MODEL, N_KV_HEADS, 2, half)
    if rope_split:
        wq = wq.transpose(0, 1, 3, 4, 2, 5)
        wk = wk.transpose(0, 1, 3, 2, 4)
    else:
        wq = wq.transpose(0, 1, 3, 2, 4, 5)
    return jnp.concatenate([wq.reshape(depth, D_MODEL, N_HEADS * HEAD_DIM),
                            wk.reshape(depth, D_MODEL, N_KV_HEADS * HEAD_DIM)], axis=2)


def _rope_tables(pos, dims):
    half = HEAD_DIM // 2
    inv = ROPE_THETA ** (-jnp.arange(half, dtype=F32) / half)
    ang = pos.astype(F32)[:, None] * inv[None, :]
    cos = jnp.cos(ang)[:, dims % half]
    sin = jnp.sin(ang)[:, dims % half] * jnp.where(dims < half, -1.0, 1.0).astype(F32)[None, :]
    return cos, sin


def _rest_weight(w_in):
    pad = jnp.zeros((w_in.shape[0], D_MODEL, LANES - GLA_GATE_RANK), w_in.dtype)
    return jnp.concatenate([w_in[:, :, OFF_V:OFF_GATES], pad], axis=2)


def kernel(x_prompt, x_sample, cache_swa_k, cache_swa_v, state_sconv, state_cconv, state_gla, norm_mix, w_in,
           conv_short, conv_conf, conv_conf_b, conf_ln_g, conf_ln_b, q_norm, k_norm, attn_sinks, w_gla_gate,
           b_gla_gate, gla_norm, w_branch, w_out, norm_ffn, w_ffn_in, w_ffn_out):
    batch, seq, _ = x_prompt.shape
    n_dec = x_sample.shape[0]
    depth = w_in.shape[0]
    bw = BRANCH_WIDTH

    dims_p, dims_s = _lane_dims(True), _lane_dims(False)
    cos_p, sin_p = _rope_tables(jnp.arange(seq), dims_p)
    cos_s, sin_s = _rope_tables(PAST_LEN + jnp.arange(1), dims_s)
    head_order = np.array([p + GROUP * side for p in range(GROUP) for side in range(2)])
    rest_widths = (LANES, 512, 512, 512, LANES)

    yp = x_prompt.reshape(batch * seq, D_MODEL)
    ys = x_sample.reshape(n_dec, D_MODEL)
    outs = {k: [] for k in ("sc_p", "cf_p", "k_p", "v_p", "S_p", "u_s", "g_s", "k_s", "v_s", "S_s")}

    rows3 = lambda a: a.reshape(depth, 1, -1)
    w_ab = w_in[:, :, :W_AB].astype(BF16)
    w_qk_p = _qk_weight(w_in, True).astype(BF16)
    w_qk_s = _qk_weight(w_in, False).astype(BF16)
    w_rest = _rest_weight(w_in).astype(BF16)
    w_gates = (w_in[:, :, OFF_GATES:].reshape(depth, D_MODEL, N_BRANCH * MERGE_CHUNKS, MERGE_COLS)
               .transpose(0, 2, 1, 3).astype(BF16))
    wbr_c = (w_branch[:, 2].reshape(depth, N_KV_HEADS, GROUP, HEAD_DIM, D_MODEL).transpose(0, 2, 1, 3, 4)
             .reshape(depth, 1, bw, D_MODEL))
    wbr = (jnp.concatenate([w_branch[:, :2], wbr_c, w_branch[:, 3:]], axis=1)
           .reshape(depth, N_BRANCH, bw, MERGE_CHUNKS, MERGE_COLS).transpose(0, 1, 3, 2, 4)
           .reshape(depth, N_BRANCH * MERGE_CHUNKS, bw, MERGE_COLS).astype(BF16))
    wout = w_out.astype(BF16)
    wfi = w_ffn_in.astype(BF16)
    wfo = w_ffn_out.astype(BF16)
    wg = jnp.concatenate([w_gla_gate, jnp.zeros((depth, LANES - GLA_GATE_RANK, GLA_HEADS * GLA_DK), F32)], axis=1).astype(BF16)
    stacked = dict(
        w_ab=w_ab, w_qk_p=w_qk_p, w_qk_s=w_qk_s, w_rest=w_rest, w_gates=w_gates, wbr=wbr, wout=wout, wfi=wfi, wfo=wfo, wg=wg,
        bg=rows3(b_gla_gate), gn=rows3(gla_norm), gm=rows3(norm_mix), gf=rows3(norm_ffn),
        qn_p=rows3(q_norm[:, dims_p]), kn_p=rows3(k_norm[:, dims_p]), qn_s=rows3(q_norm[:, dims_s]), kn_s=rows3(k_norm[:, dims_s]),
        sinks_s=jnp.broadcast_to(attn_sinks[:, :, None], (depth, N_HEADS, LANES)),
        sinks_p=jnp.broadcast_to(attn_sinks[:, head_order][:, :, None, None], (depth, N_HEADS, 1, LANES)),
        cs=conv_short, ccb=rows3(conv_conf_b), lng=rows3(conf_ln_g), lnb=rows3(conf_ln_b),
        cc=jnp.broadcast_to(conv_conf[:, :, None, :], (depth, CF_CONV_W, SUBLANES, bw)),
    )
    sct = state_sconv.transpose(0, 2, 1, 3)
    cft = state_cconv.transpose(0, 2, 1, 3)
    kc_s = cache_swa_k.reshape(depth, n_dec, WINDOW, LANES)
    vc_s = cache_swa_v.reshape(depth, n_dec, WINDOW, LANES)
    s0 = state_gla.reshape(depth, n_dec, GLA_HEADS * GLA_DK, GLA_DV)

    for l in range(depth):
        p = {name: _Layer(arr, l) for name, arr in stacked.items()}

        qk, v, gqk, gv, gg, ga = _inproj(yp, p["gm"], (p["w_qk_p"], p["w_rest"]), ((640,), rest_widths), 1024)
        oc, kc, vc = _swa_prompt(qk, v, cos_p, sin_p, p["qn_p"], p["kn_p"], p["sinks_p"], batch, seq, 4)
        od, st = _gla_prompt(gqk, gv, gg, ga, p["wg"], p["bg"], p["gn"], batch, seq, 512)
        x1, sc_new, cf_new = _mix_prompt(yp, oc, od, p["gm"], p["w_ab"], p["cs"], p["cc"], p["ccb"], p["lng"], p["lnb"],
                                         p["w_gates"], p["wbr"], p["wout"], batch, seq, 512)
        yp = _ffn(x1, p["gf"], p["wfi"], p["wfo"], 1024)
        outs["sc_p"].append(sc_new)
        outs["cf_p"].append(cf_new)
        outs["k_p"].append(kc)
        outs["v_p"].append(vc)
        outs["S_p"].append(st)

        ab_s, qk_s, v_s, gqk_s, gv_s, gg_s, ga_s = _inproj(
            ys, p["gm"], (p["w_ab"], p["w_qk_s"], p["w_rest"]), ((W_AB,), (640,), rest_widths), n_dec)
        small = (p["cs"], p["cc"], p["ccb"], p["lng"], p["lnb"], p["qn_s"], p["kn_s"], cos_s, sin_s, p["sinks_s"],
                 p["wg"], p["bg"], p["gn"])
        oa_s, ob_s, oc_s, od_s, u_s, g_s, kh_s, s_new = _sample_mixers(
            ab_s, qk_s, v_s, gqk_s, gv_s, gg_s, ga_s, sct, cft, kc_s, vc_s, s0, small, 8, l)
        ys = _merge(ys, oa_s, ob_s, oc_s, od_s, p["gm"], p["w_gates"], p["wbr"], p["wout"], n_dec)
        ys = _ffn(ys, p["gf"], p["wfi"], p["wfo"], n_dec)
        outs["u_s"].append(u_s)
        outs["g_s"].append(g_s)
        outs["k_s"].append(kh_s)
        outs["v_s"].append(v_s)
        outs["S_s"].append(s_new)

    st = {k: jnp.stack(v) for k, v in outs.items()}
    sc_p = st["sc_p"].reshape(depth, batch, U_PAD, bw)[:, :, U_PAD - (SC_CONV_W - 1):]
    cf_p = st["cf_p"].reshape(depth, batch, G_PAD, bw)[:, :, G_PAD - (CF_CONV_W - 1):]
    k_p = (st["k_p"].reshape(depth, batch, WINDOW, 2, N_KV_HEADS, HEAD_DIM // 2).transpose(0, 1, 2, 4, 3, 5)
           .reshape(depth, batch, WINDOW, N_KV_HEADS, HEAD_DIM))
    v_p = st["v_p"].reshape(depth, batch, WINDOW, N_KV_HEADS, HEAD_DIM)
    sp = st["S_p"].reshape(depth, batch, GLA_HEADS // 2, 2, GLA_DV, 2, GLA_DK)
    S_p = (jnp.stack([sp[:, :, :, 0, :, 0], sp[:, :, :, 1, :, 1]], axis=3)
           .reshape(depth, batch, GLA_HEADS, GLA_DV, GLA_DK).transpose(0, 1, 2, 4, 3))
    sc_s = jnp.concatenate([state_sconv[:, :, 1:], st["u_s"][:, :, None]], axis=2)
    cf_s = jnp.concatenate([state_cconv[:, :, 1:], st["g_s"][:, :, None]], axis=2)
    k_s = jnp.concatenate([cache_swa_k[:, :, 1:], st["k_s"].reshape(depth, n_dec, 1, N_KV_HEADS, HEAD_DIM)], axis=2)
    v_s = jnp.concatenate([cache_swa_v[:, :, 1:], st["v_s"].reshape(depth, n_dec, 1, N_KV_HEADS, HEAD_DIM)], axis=2)
    S_s = st["S_s"].reshape(depth, n_dec, GLA_HEADS, GLA_DK, GLA_DV)
    return (yp.reshape(batch, seq, D_MODEL), ys.reshape(n_dec, 1, D_MODEL),
            sc_p, sc_s, cf_p, cf_s, k_p, k_s, v_p, v_s, S_p, S_s)
```

```python
import functools

import numpy as np
import jax
import jax.numpy as jnp
from jax import lax
from jax.experimental import pallas as pl
from jax.experimental.pallas import tpu as pltpu

F32 = jnp.float32
BF16 = jnp.bfloat16

D_MODEL = 1024
PAST_LEN = 16384
N_BRANCH = 4
BRANCH_WIDTH = D_MODEL // 2
SC_CONV_W = 3
CF_CONV_W = 31
N_HEADS = 8
N_KV_HEADS = 2
HEAD_DIM = 64
GROUP = N_HEADS // N_KV_HEADS
WINDOW = 128
ATTN_BLOCK = 128
ROPE_THETA = 10000.0
GLA_HEADS = 4
GLA_DK = 64
GLA_DV = BRANCH_WIDTH // GLA_HEADS
GLA_GATE_RANK = 16
GLA_GATE_NORM = 16.0
GLA_CHUNK = 64
D_FF = -(-8 * D_MODEL // (3 * 256)) * 256
NEG_INF = -1e30
RMS_EPS = 1e-6
LN_EPS = 1e-5

LANES = 128
SUBLANES = 8
FF_CHUNK = 256
VMEM_LIMIT = 56 * 1024 * 1024

TM_INPROJ = 1024
SWA_BLOCKS = 4
TG_GLA = 512
TM_MIX = 512
TM_FFN = 1024
SB_SAMPLE = 8

OFF_AB = 0
W_AB = 5 * BRANCH_WIDTH
OFF_Q = W_AB
OFF_K = OFF_Q + N_HEADS * HEAD_DIM
OFF_V = OFF_K + N_KV_HEADS * HEAD_DIM
OFF_GQ = OFF_V + N_KV_HEADS * HEAD_DIM
OFF_GK = OFF_GQ + GLA_HEADS * GLA_DK
OFF_GV = OFF_GK + GLA_HEADS * GLA_DK
OFF_GG = OFF_GV + GLA_HEADS * GLA_DV
OFF_GA = OFF_GG + GLA_HEADS * GLA_DV
OFF_GATES = OFF_GA + GLA_GATE_RANK


class _Layer:
    def __init__(self, stacked, layer):
        self.stacked, self.layer = stacked, layer


def _consts(params):
    specs, operands = [], []
    for p in params:
        if isinstance(p, _Layer):
            shape, layer = p.stacked.shape[1:], p.layer
            specs.append(pl.BlockSpec((None,) + shape, lambda *_, layer=layer, nd=len(shape): (layer,) + (0,) * nd,
                                      pipeline_mode=pl.Buffered(1)))
            operands.append(p.stacked)
        else:
            specs.append(pl.BlockSpec(p.shape, lambda *_, nd=p.ndim: (0,) * nd, pipeline_mode=pl.Buffered(1)))
            operands.append(p)
    return specs, operands


def _params(n_axes, flags=None):
    return pltpu.CompilerParams(dimension_semantics=("arbitrary",) * n_axes, vmem_limit_bytes=VMEM_LIMIT, flags=flags)


def _rms(x, g):
    return x * lax.rsqrt(jnp.mean(x * x, axis=-1, keepdims=True) + RMS_EPS) * g


def _dot(a, b):
    return jnp.dot(a, b, preferred_element_type=F32)


def _dot_nt(a, b):
    return lax.dot_general(a, b, (((1,), (1,)), ((), ())), preferred_element_type=F32)


def _dot_tn(a, b):
    return lax.dot_general(a, b, (((0,), (0,)), ((), ())), preferred_element_type=F32)


def _log_sigmoid(z):
    return -(jnp.maximum(-z, 0.0) + jnp.log1p(jnp.exp(-jnp.abs(z))))


def _silu(x):
    return x * jax.nn.sigmoid(x)


def _inproj_body(x_ref, g_ref, *refs, widths):
    w_refs, out_refs = refs[:len(widths)], list(refs[len(widths):])
    xn = _rms(x_ref[...], g_ref[...]).astype(BF16)
    for w_ref, group in zip(w_refs, widths):
        off = 0
        for wd in group:
            out_refs.pop(0)[...] = _dot(xn, w_ref[:, off:off + wd])
            off += wd


def _inproj(x2d, g, weights, widths, tm):
    n = x2d.shape[0]
    flat = [wd for group in widths for wd in group]
    const_specs, const_ops = _consts((g,) + tuple(weights))
    return pl.pallas_call(
        functools.partial(_inproj_body, widths=widths),
        grid=(n // tm,),
        in_specs=[pl.BlockSpec((tm, D_MODEL), lambda i: (i, 0))] + const_specs,
        out_specs=[pl.BlockSpec((tm, wd), lambda i: (i, 0)) for wd in flat],
        out_shape=[jax.ShapeDtypeStruct((n, wd), F32) for wd in flat],
        compiler_params=_params(1),
        name="inproj",
    )(x2d, *const_ops)


def _swa_body(qk_ref, v_ref, cos_ref, sin_ref, qn_ref, kn_ref, sink_ref, oc_ref, kc_ref, vc_ref, kprev, vprev, *, nblk):
    j = pl.program_id(1)
    tq = ATTN_BLOCK

    @pl.when(j == 0)
    def _():
        kprev[...] = jnp.zeros_like(kprev)
        vprev[...] = jnp.zeros_like(vprev)

    blk = qk_ref[...]
    cos = cos_ref[...]
    sin = sin_ref[...]
    lane = lax.broadcasted_iota(jnp.int32, (1, LANES), 1)
    left = (lane // 32) % 2 == 0
    lo64 = lane < 64

    def norm_rope(x, g):
        x2 = x * x
        ssl = jnp.sum(jnp.where(left, x2, 0.0), axis=-1, keepdims=True)
        ssr = jnp.sum(jnp.where(left, 0.0, x2), axis=-1, keepdims=True)
        r = jnp.where(left, lax.rsqrt(ssl * (1.0 / HEAD_DIM) + RMS_EPS), lax.rsqrt(ssr * (1.0 / HEAD_DIM) + RMS_EPS))
        y = x * r * g
        return y * cos + pltpu.roll(y, 64, 1) * sin

    khat = norm_rope(blk[:, 512:640], kn_ref[...])
    v = v_ref[...]
    k_all = jnp.concatenate([kprev[...], khat], axis=0).astype(BF16)
    v_all = jnp.concatenate([vprev[...], v], axis=0).astype(BF16)
    qhat = [norm_rope(blk[:, p * LANES:(p + 1) * LANES], qn_ref[...]) * (HEAD_DIM ** -0.5) for p in range(GROUP)]
    upper = lax.broadcasted_iota(jnp.int32, (tq, tq), 1) > lax.broadcasted_iota(jnp.int32, (tq, tq), 0)
    sink = sink_ref[...][:, :, 0:1]

    for i in range(nblk):
        rows = slice(i * tq, (i + 1) * tq)
        qs = []
        for p in range(GROUP):
            qs.append(jnp.where(left, qhat[p][rows], 0.0))
            qs.append(jnp.where(left, 0.0, qhat[p][rows]))
        q_all = jnp.concatenate(qs, axis=0).astype(BF16)
        s2 = _dot_nt(q_all, k_all[i * tq:(i + 2) * tq]).reshape(N_HEADS, tq, 2 * tq)
        s_prev = s2[:, :, :tq]
        if i == 0:
            s_prev = s_prev + jnp.where(j > 0, 0.0, NEG_INF)
        s = jnp.where(upper, s_prev, s2[:, :, tq:])
        m = jnp.maximum(jnp.max(s, axis=-1, keepdims=True), sink)
        e = jnp.exp(s - m)
        inv = 1.0 / (jnp.sum(e, axis=-1, keepdims=True) + jnp.exp(sink - m))
        e2 = jnp.concatenate([jnp.where(upper, e, 0.0), jnp.where(upper, 0.0, e)], axis=-1).astype(BF16)
        o = _dot(e2.reshape(N_HEADS * tq, 2 * tq), v_all[i * tq:(i + 2) * tq]).reshape(N_HEADS, tq, LANES) * inv
        for p in range(GROUP):
            oc_ref[rows, p * LANES:(p + 1) * LANES] = jnp.where(lo64, o[2 * p], o[2 * p + 1]).astype(oc_ref.dtype)

    last = slice((nblk - 1) * tq, nblk * tq)
    kprev[...] = khat[last]
    vprev[...] = v[last]
    kc_ref[...] = khat[last]
    vc_ref[...] = v[last]


def _swa_prompt(qk, vga, cos, sin, qn, kn, sinks, batch, seq, nblk):
    ts = nblk * ATTN_BLOCK
    nb = seq // ts
    row = lambda b, j: (b * nb + j, 0)
    const_specs, const_ops = _consts((qn, kn, sinks))
    return pl.pallas_call(
        functools.partial(_swa_body, nblk=nblk),
        grid=(batch, nb),
        in_specs=[pl.BlockSpec((ts, 640), row), pl.BlockSpec((ts, LANES), row),
                  pl.BlockSpec((ts, LANES), lambda b, j: (j, 0)),
                  pl.BlockSpec((ts, LANES), lambda b, j: (j, 0))] + const_specs,
        out_specs=[pl.BlockSpec((ts, 512), row),
                   pl.BlockSpec((ATTN_BLOCK, LANES), lambda b, j: (b, 0)),
                   pl.BlockSpec((ATTN_BLOCK, LANES), lambda b, j: (b, 0))],
        out_shape=[jax.ShapeDtypeStruct((batch * seq, 512), BF16),
                   jax.ShapeDtypeStruct((batch * WINDOW, LANES), F32),
                   jax.ShapeDtypeStruct((batch * WINDOW, LANES), F32)],
        scratch_shapes=[pltpu.VMEM((ATTN_BLOCK, LANES), F32), pltpu.VMEM((ATTN_BLOCK, LANES), F32)],
        compiler_params=_params(2),
        name="swa_prompt",
    )(qk, vga, cos, sin, *const_ops)


def _split2(x):
    hi = x.astype(BF16)
    lo = (x - hi.astype(F32)).astype(BF16)
    return hi, lo


def _gla_body(gqk_ref, gv_ref, gg_ref, ga_ref, wg_ref, bg_ref, gn_ref, od_ref, st_ref, st_scr, *, tg):
    t = pl.program_id(1)
    ck = GLA_CHUNK

    @pl.when(t == 0)
    def _():
        st_scr[...] = jnp.zeros_like(st_scr)

    z = _dot(ga_ref[...].astype(BF16), wg_ref[...]) + bg_ref[...]
    la = _log_sigmoid(z) * (1.0 / GLA_GATE_NORM)
    r_i = lax.broadcasted_iota(jnp.int32, (tg, tg), 0)
    c_i = lax.broadcasted_iota(jnp.int32, (tg, tg), 1)
    same = (r_i // ck) == (c_i // ck)
    tri = jnp.where(same & (c_i <= r_i), 1.0, 0.0).astype(BF16)
    ones = jnp.where(same, 1.0, 0.0).astype(BF16)
    la_hi, la_lo = _split2(la)
    bcum = _dot(tri, la_hi) + _dot(tri, la_lo)
    btot = _dot(ones, la_hi) + _dot(ones, la_lo)

    gqk = gqk_ref[...]
    nk = GLA_HEADS * GLA_DK
    gk = gqk[:, nk:]
    qe = (gqk[:, :nk] * (GLA_DK ** -0.5)) * jnp.exp(bcum)
    ke = gk * jnp.exp(-bcum)
    kd = gk * jnp.exp(btot - bcum)
    dec = jnp.exp(btot)

    lane = lax.broadcasted_iota(jnp.int32, (1, LANES), 1)
    lo64 = lane < 64
    causal = lax.broadcasted_iota(jnp.int32, (ck, ck), 1) <= lax.broadcasted_iota(jnp.int32, (ck, ck), 0)
    u_row = lax.broadcasted_iota(jnp.int32, (2 * GLA_DV, LANES), 0)
    u_lane = lax.broadcasted_iota(jnp.int32, (2 * GLA_DV, LANES), 1)
    diag_blocks = (u_row < GLA_DV) == (u_lane < GLA_DK)
    gn = gn_ref[...]

    for c in range(tg // ck):
        rows = slice(c * ck, (c + 1) * ck)
        for p in range(GLA_HEADS // 2):
            lanes = slice(p * LANES, (p + 1) * LANES)
            vcols = slice(p * 2 * GLA_DV, (p + 1) * 2 * GLA_DV)
            qe_p = qe[rows, lanes]
            ke_p = ke[rows, lanes].astype(BF16)
            kd_p = kd[rows, lanes].astype(BF16)
            v_p = gv_ref[rows, vcols].astype(BF16)
            st = st_scr[p]
            o_halves = []
            for side in range(2):
                qm = jnp.where(lo64 == (side == 0), qe_p, 0.0).astype(BF16)
                a = jnp.where(causal, _dot_nt(qm, ke_p), 0.0).astype(BF16)
                o_halves.append(_dot(a, v_p[:, side * GLA_DV:(side + 1) * GLA_DV]))
            o = jnp.concatenate(o_halves, axis=1) + _dot_nt(qe_p.astype(BF16), st.astype(BF16))
            upd = jnp.where(diag_blocks, _dot_tn(v_p, kd_p), 0.0)
            st_scr[p] = dec[c * ck:c * ck + 1, lanes] * st + upd
            gate = gg_ref[rows, vcols]
            for side in range(2):
                hs = slice(side * GLA_DV, (side + 1) * GLA_DV)
                od_ref[rows, p * 2 * GLA_DV + side * GLA_DV:p * 2 * GLA_DV + (side + 1) * GLA_DV] = (
                    _rms(o[:, hs], gn) * _silu(gate[:, hs])).astype(od_ref.dtype)

    st_ref[...] = st_scr[...]


def _gla_prompt(gqk, gv, gg, vga, wg, bg, gn, batch, seq, tg):
    nt = seq // tg
    row = lambda b, t: (b * nt + t, 0)
    ga_row = lambda b, t: (b * nt + t, 1)
    const_specs, const_ops = _consts((wg, bg, gn))
    return pl.pallas_call(
        functools.partial(_gla_body, tg=tg),
        grid=(batch, nt),
        in_specs=[pl.BlockSpec((tg, 512), row), pl.BlockSpec((tg, 512), row), pl.BlockSpec((tg, 512), row),
                  pl.BlockSpec((tg, LANES), ga_row)] + const_specs,
        out_specs=[pl.BlockSpec((tg, 512), row),
                   pl.BlockSpec((None, 2, 2 * GLA_DV, LANES), lambda b, t: (b, 0, 0, 0))],
        out_shape=[jax.ShapeDtypeStruct((batch * seq, 512), BF16),
                   jax.ShapeDtypeStruct((batch, 2, 2 * GLA_DV, LANES), F32)],
        scratch_shapes=[pltpu.VMEM((2, 2 * GLA_DV, LANES), F32)],
        compiler_params=_params(2),
        name="gla_prompt",
    )(gqk, gv, gg, vga, *const_ops)


U_PAD = 8
G_PAD = 32
CONV_ROWS = 16


def _conv_prepare(xn, w_ref, cs_ref, ubuf, gbuf, gsh, oa_scr, tm):
    i = pl.program_id(1)
    bw = BRANCH_WIDTH

    @pl.when(i == 0)
    def _():
        ubuf[0:U_PAD, :] = jnp.zeros((U_PAD, bw), F32)
        gbuf[0:G_PAD, :] = jnp.zeros((G_PAD, bw), F32)

    @pl.when(i > 0)
    def _():
        ubuf[0:U_PAD, :] = ubuf[tm:tm + U_PAD, :]
        gbuf[0:G_PAD, :] = gbuf[tm:tm + G_PAD, :]

    sc_h = _dot(xn, w_ref[:, 0:bw])
    sc_c = _dot(xn, w_ref[:, 2 * bw:3 * bw])
    u = sc_c * sc_h
    ubuf[U_PAD:U_PAD + tm, :] = u
    conv_u = (cs_ref[0:1, :] * ubuf[U_PAD - 2:U_PAD - 2 + tm, :] + cs_ref[1:2, :] * ubuf[U_PAD - 1:U_PAD - 1 + tm, :]
              + cs_ref[2:3, :] * u)
    oa_scr[...] = (_dot(xn, w_ref[:, bw:2 * bw]) * conv_u).astype(oa_scr.dtype)

    cf_a = _dot(xn, w_ref[:, 3 * bw:4 * bw])
    cf_b = _dot(xn, w_ref[:, 4 * bw:5 * bw])
    gbuf[G_PAD:G_PAD + tm, :] = cf_a * jax.nn.sigmoid(cf_b)
    span = tm + G_PAD - SUBLANES
    for sh in range(1, SUBLANES):
        gsh[sh - 1] = gbuf[sh:sh + span, :]


def _conv_block(r0, cc_ref, gbuf, gsh, cbuf):
    base = G_PAD - (CF_CONV_W - 1)
    acc = None
    for k in range(CF_CONV_W):
        al, sh = divmod(base + k, SUBLANES)
        rows = pl.ds(pl.multiple_of(r0 + al * SUBLANES, SUBLANES), CONV_ROWS)
        w_tap = jnp.concatenate([cc_ref[k]] * (CONV_ROWS // SUBLANES), axis=0)
        tap = w_tap * (gbuf[rows, :] if sh == 0 else gsh[sh - 1, rows, :])
        acc = tap if acc is None else acc + tap
    cbuf[pl.ds(pl.multiple_of(r0, CONV_ROWS), CONV_ROWS), :] = acc


def _conv_finish(cbuf, ccb_ref, lng_ref, lnb_ref):
    c = cbuf[...] + ccb_ref[...]
    mu = jnp.mean(c, axis=-1, keepdims=True)
    xc = c - mu
    y = xc * lax.rsqrt(jnp.mean(xc * xc, axis=-1, keepdims=True) + LN_EPS) * lng_ref[...] + lnb_ref[...]
    return _silu(y)


MERGE_COLS = 256
MERGE_CHUNKS = D_MODEL // MERGE_COLS


def _merge_term(xn, o, r, c, wgt_ref, wbr_ref):
    idx = r * MERGE_CHUNKS + c
    return jax.nn.sigmoid(_dot(xn, wgt_ref[idx])) * _dot(o, wbr_ref[idx])


def _merge_out(x, xn, branches, wgt_ref, wbr_ref, wout_ref):
    y = x
    for c in range(MERGE_CHUNKS):
        merged = None
        for r, o in enumerate(branches):
            term = _merge_term(xn, o.astype(BF16), r, c, wgt_ref, wbr_ref)
            merged = term if merged is None else merged + term
        y = y + _dot(merged.astype(BF16), wout_ref[c * MERGE_COLS:(c + 1) * MERGE_COLS, :])
    return y


def _swiglu(x1, gf_ref, wfi_ref, wfo_ref):
    xn2 = _rms(x1, gf_ref[...]).astype(BF16)
    y = x1
    for c0 in range(0, D_FF, FF_CHUNK):
        h_gate = _dot(xn2, wfi_ref[:, c0:c0 + FF_CHUNK])
        h_up = _dot(xn2, wfi_ref[:, D_FF + c0:D_FF + c0 + FF_CHUNK])
        y = y + _dot((_silu(h_gate) * h_up).astype(BF16), wfo_ref[c0:c0 + FF_CHUNK, :])
    return y


def _mix_body(x_ref, oc_ref, od_ref, gm_ref, wab_ref, cs_ref, cc_ref, ccb_ref, lng_ref, lnb_ref,
              wgt_ref, wbr_ref, wout_ref, x1_ref, sc_ref, cf_ref, ubuf, gbuf, gsh, cbuf, xn_scr, oa_scr, macc, *, tm):
    xn_scr[...] = _rms(x_ref[...], gm_ref[...]).astype(BF16)
    _conv_prepare(xn_scr[...], wab_ref, cs_ref, ubuf, gbuf, gsh, oa_scr, tm)

    rows_per_step = tm // MERGE_CHUNKS

    def step(c, carry):
        xn = xn_scr[...]
        macc[c] = (_merge_term(xn, oc_ref[...], 2, c, wgt_ref, wbr_ref) + _merge_term(xn, od_ref[...], 3, c, wgt_ref, wbr_ref)
                   + _merge_term(xn, oa_scr[...], 0, c, wgt_ref, wbr_ref))
        row0 = c * rows_per_step
        for r0 in range(0, rows_per_step, CONV_ROWS):
            _conv_block(row0 + r0, cc_ref, gbuf, gsh, cbuf)
        return carry

    lax.fori_loop(0, MERGE_CHUNKS, step, 0, unroll=2)

    o_b = _conv_finish(cbuf, ccb_ref, lng_ref, lnb_ref).astype(BF16)
    xn = xn_scr[...]
    y = x_ref[...]
    for c in range(MERGE_CHUNKS):
        merged = macc[c] + _merge_term(xn, o_b, 1, c, wgt_ref, wbr_ref)
        y = y + _dot(merged.astype(BF16), wout_ref[c * MERGE_COLS:(c + 1) * MERGE_COLS, :])
    x1_ref[...] = y
    sc_ref[...] = ubuf[tm:tm + U_PAD, :]
    cf_ref[...] = gbuf[tm:tm + G_PAD, :]


def _mix_prompt(x2d, oc, od, gm, w_ab, cs, cc, ccb, lng, lnb, wgt, wbr, wout, batch, seq, tm):
    nt = seq // tm
    row = lambda b, i: (b * nt + i, 0)
    bw = BRANCH_WIDTH
    const_specs, const_ops = _consts((gm, w_ab, cs, cc, ccb, lng, lnb, wgt, wbr, wout))
    return pl.pallas_call(
        functools.partial(_mix_body, tm=tm),
        grid=(batch, nt),
        in_specs=[pl.BlockSpec((tm, D_MODEL), row), pl.BlockSpec((tm, bw), row), pl.BlockSpec((tm, bw), row)]
                 + const_specs,
        out_specs=[pl.BlockSpec((tm, D_MODEL), row),
                   pl.BlockSpec((U_PAD, bw), lambda b, i: (b, 0)), pl.BlockSpec((G_PAD, bw), lambda b, i: (b, 0))],
        out_shape=[jax.ShapeDtypeStruct((batch * seq, D_MODEL), F32),
                   jax.ShapeDtypeStruct((batch * U_PAD, bw), F32), jax.ShapeDtypeStruct((batch * G_PAD, bw), F32)],
        scratch_shapes=[pltpu.VMEM((U_PAD + tm, bw), F32), pltpu.VMEM((G_PAD + tm, bw), F32),
                        pltpu.VMEM((SUBLANES - 1, tm + G_PAD - SUBLANES, bw), F32), pltpu.VMEM((tm, bw), F32),
                        pltpu.VMEM((tm, D_MODEL), BF16), pltpu.VMEM((tm, bw), BF16),
                        pltpu.VMEM((MERGE_CHUNKS, tm, MERGE_COLS), F32)],
        compiler_params=_params(2),
        name="mix_prompt",
    )(x2d, oc, od, *const_ops)


def _ffn_body(x1_ref, gf_ref, wfi_ref, wfo_ref, y_ref):
    y_ref[...] = _swiglu(x1_ref[...], gf_ref, wfi_ref, wfo_ref)


def _ffn(x2d, gf, wfi, wfo, tm):
    n = x2d.shape[0]
    row = lambda i: (i, 0)
    const_specs, const_ops = _consts((gf, wfi, wfo))
    return pl.pallas_call(
        _ffn_body,
        grid=(n // tm,),
        in_specs=[pl.BlockSpec((tm, D_MODEL), row)] + const_specs,
        out_specs=pl.BlockSpec((tm, D_MODEL), row),
        out_shape=jax.ShapeDtypeStruct((n, D_MODEL), F32),
        compiler_params=_params(1),
        name="ffn",
    )(x2d, *const_ops)


def _merge_body(x_ref, oa_ref, ob_ref, oc_ref, od_ref, gm_ref, wgt_ref, wbr_ref, wout_ref, x1_ref):
    x = x_ref[...]
    xn = _rms(x, gm_ref[...]).astype(BF16)
    x1_ref[...] = _merge_out(x, xn, (oa_ref[...], ob_ref[...], oc_ref[...], od_ref[...]), wgt_ref, wbr_ref, wout_ref)


def _merge(x2d, oa, ob, oc, od, gm, wgt, wbr, wout, tm):
    n = x2d.shape[0]
    bw = BRANCH_WIDTH
    row = lambda i: (i, 0)
    const_specs, const_ops = _consts((gm, wgt, wbr, wout))
    return pl.pallas_call(
        _merge_body,
        grid=(n // tm,),
        in_specs=[pl.BlockSpec((tm, D_MODEL), row)] + [pl.BlockSpec((tm, bw), row)] * 4 + const_specs,
        out_specs=pl.BlockSpec((tm, D_MODEL), row),
        out_shape=jax.ShapeDtypeStruct((n, D_MODEL), F32),
        compiler_params=_params(1),
        name="merge",
    )(x2d, oa, ob, oc, od, *const_ops)


def _sample_body(ab_ref, qk_ref, v_ref, gqk_ref, gv_ref, gg_ref, ga_ref, sct_ref, cft_ref, kc_ref, vc_ref, s_ref,
                 cs_ref, cc_ref, ccb_ref, lng_ref, lnb_ref, qn_ref, kn_ref, cos_ref, sin_ref, sink_ref,
                 wg_ref, bg_ref, gn_ref,
                 oa_ref, ob_ref, oc_ref, od_ref, u_ref, g_ref, kh_ref, sn_ref,
                 qh_scr, gl_scr, *, sb):
    bw = BRANCH_WIDTH
    sc_h = ab_ref[:, 0:bw]
    sc_b = ab_ref[:, bw:2 * bw]
    sc_c = ab_ref[:, 2 * bw:3 * bw]
    u = sc_c * sc_h
    u_ref[...] = u
    oa_ref[...] = sc_b * (cs_ref[0:1, :] * sct_ref[0] + cs_ref[1:2, :] * sct_ref[1] + cs_ref[2:3, :] * u)
    g = ab_ref[:, 3 * bw:4 * bw] * jax.nn.sigmoid(ab_ref[:, 4 * bw:5 * bw])
    g_ref[...] = g
    c = ccb_ref[...] + cc_ref[CF_CONV_W - 1, 0:1, :] * g
    for k in range(CF_CONV_W - 1):
        c = c + cc_ref[k, 0:1, :] * cft_ref[k]
    mu = jnp.mean(c, axis=-1, keepdims=True)
    xc = c - mu
    ob_ref[...] = _silu(xc * lax.rsqrt(jnp.mean(xc * xc, axis=-1, keepdims=True) + LN_EPS) * lng_ref[...] + lnb_ref[...])

    lane = lax.broadcasted_iota(jnp.int32, (1, LANES), 1)
    lo64 = lane < 64
    first32 = (lane % 64) < 32
    cos = cos_ref[...]
    sin = sin_ref[...]

    def norm_rope(x, gw):
        x2 = x * x
        ssl = jnp.sum(jnp.where(lo64, x2, 0.0), axis=-1, keepdims=True)
        ssr = jnp.sum(jnp.where(lo64, 0.0, x2), axis=-1, keepdims=True)
        r = jnp.where(lo64, lax.rsqrt(ssl * (1.0 / HEAD_DIM) + RMS_EPS), lax.rsqrt(ssr * (1.0 / HEAD_DIM) + RMS_EPS))
        y = x * r * gw
        partner = jnp.where(first32, pltpu.roll(y, LANES - 32, 1), pltpu.roll(y, 32, 1))
        return y * cos + partner * sin

    for p in range(GROUP):
        qh_scr[:, p * LANES:(p + 1) * LANES] = norm_rope(qk_ref[:, p * LANES:(p + 1) * LANES], qn_ref[...])
    khat = norm_rope(qk_ref[:, 512:640], kn_ref[...])
    kh_ref[...] = khat
    qh_scr[:, 512:640] = khat

    z = _dot(ga_ref[...].astype(BF16), wg_ref[...]) + bg_ref[...]
    nk = GLA_HEADS * GLA_DK
    gl_scr[:, 0:nk] = jnp.exp(_log_sigmoid(z) * (1.0 / GLA_GATE_NORM))
    gl_scr[:, nk:2 * nk] = gqk_ref[:, 0:nk] * (GLA_DK ** -0.5)
    gl_scr[:, 2 * nk:3 * nk] = gqk_ref[:, nk:2 * nk]

    row_i = lax.broadcasted_iota(jnp.int32, (LANES, LANES), 0)
    eye = row_i == lax.broadcasted_iota(jnp.int32, (LANES, LANES), 1)
    top = row_i < GLA_DK
    key_bias = jnp.where(lax.broadcasted_iota(jnp.int32, (WINDOW, 1), 0) == 0, NEG_INF, 0.0)

    def to_col(row):
        return jnp.sum(jnp.where(eye, jnp.broadcast_to(row, (LANES, LANES)), 0.0), axis=-1, keepdims=True)

    for b in range(sb):
        kmat = kc_ref[b]
        vmat = vc_ref[b]
        k_new = qh_scr[b:b + 1, 512:640]
        v_new = v_ref[b:b + 1, :]
        for p in range(GROUP):
            q_row = qh_scr[b:b + 1, p * LANES:(p + 1) * LANES]
            prod = kmat * q_row
            pn = k_new * q_row
            outs = []
            for side in range(2):
                sel = lo64 == (side == 0)
                s = jnp.sum(jnp.where(sel, prod, 0.0), axis=-1, keepdims=True) * (HEAD_DIM ** -0.5) + key_bias
                s_n = jnp.sum(jnp.where(sel, pn, 0.0), axis=-1, keepdims=True) * (HEAD_DIM ** -0.5)
                sink = sink_ref[p + GROUP * side:p + GROUP * side + 1, 0:1]
                m = jnp.maximum(jnp.maximum(jnp.max(s, axis=0, keepdims=True), s_n), sink)
                e = jnp.exp(s - m)
                e_n = jnp.exp(s_n - m)
                denom = jnp.sum(e, axis=0, keepdims=True) + e_n + jnp.exp(sink - m)
                outs.append((jnp.sum(e * vmat, axis=0, keepdims=True) + e_n * v_new) / denom)
            oc_ref[b:b + 1, p * LANES:(p + 1) * LANES] = jnp.where(lo64, outs[0], outs[1])

        for p in range(GLA_HEADS // 2):
            lanes = slice(p * LANES, (p + 1) * LANES)
            a_col = to_col(gl_scr[b:b + 1, p * LANES:(p + 1) * LANES])
            q_col = to_col(gl_scr[b:b + 1, nk + p * LANES:nk + (p + 1) * LANES])
            k_col = to_col(gl_scr[b:b + 1, 2 * nk + p * LANES:2 * nk + (p + 1) * LANES])
            v0 = gv_ref[b:b + 1, 2 * p * GLA_DV:(2 * p + 1) * GLA_DV]
            v1 = gv_ref[b:b + 1, (2 * p + 1) * GLA_DV:(2 * p + 2) * GLA_DV]
            vsel = jnp.where(top, jnp.broadcast_to(v0, (LANES, GLA_DV)), jnp.broadcast_to(v1, (LANES, GLA_DV)))
            s_new = a_col * s_ref[b, lanes, :] + k_col * vsel
            sn_ref[b, lanes, :] = s_new
            w = q_col * s_new
            od_ref[b:b + 1, 2 * p * GLA_DV:(2 * p + 1) * GLA_DV] = jnp.sum(w[0:GLA_DK], axis=0, keepdims=True)
            od_ref[b:b + 1, (2 * p + 1) * GLA_DV:(2 * p + 2) * GLA_DV] = jnp.sum(w[GLA_DK:], axis=0, keepdims=True)

    gn = gn_ref[...]
    for h in range(GLA_HEADS):
        hs = slice(h * GLA_DV, (h + 1) * GLA_DV)
        od_ref[:, hs] = _rms(od_ref[:, hs], gn) * _silu(gg_ref[:, hs])


def _sample_mixers(ab, qk, vga, gqk, gv, gg, sct, cft, kc, vc, s0, small, sb, layer):
    n = ab.shape[0]
    bw = BRANCH_WIDTH
    row = lambda i: (i, 0)
    ga_row = lambda i: (i, 1)
    lead3 = lambda i: (i, 0, 0)
    state_lead = lambda i: (layer, i, 0, 0)
    state_mid = lambda i: (layer, 0, i, 0)
    const_specs, const_ops = _consts(small)
    return pl.pallas_call(
        functools.partial(_sample_body, sb=sb),
        grid=(n // sb,),
        in_specs=[pl.BlockSpec((sb, W_AB), row), pl.BlockSpec((sb, 640), row), pl.BlockSpec((sb, LANES), row),
                  pl.BlockSpec((sb, 512), row),
                  pl.BlockSpec((sb, 512), row), pl.BlockSpec((sb, 512), row), pl.BlockSpec((sb, LANES), ga_row),
                  pl.BlockSpec((None, SC_CONV_W - 1, sb, bw), state_mid),
                  pl.BlockSpec((None, CF_CONV_W - 1, sb, bw), state_mid),
                  pl.BlockSpec((None, sb, WINDOW, LANES), state_lead), pl.BlockSpec((None, sb, WINDOW, LANES), state_lead),
                  pl.BlockSpec((None, sb, GLA_HEADS * GLA_DK, GLA_DV), state_lead)] + const_specs,
        out_specs=[pl.BlockSpec((sb, bw), row)] * 6 + [pl.BlockSpec((sb, LANES), row),
                                                       pl.BlockSpec((sb, GLA_HEADS * GLA_DK, GLA_DV), lead3)],
        out_shape=[jax.ShapeDtypeStruct((n, bw), F32)] * 6 + [jax.ShapeDtypeStruct((n, LANES), F32),
                                                              jax.ShapeDtypeStruct(s0.shape[1:], F32)],
        scratch_shapes=[pltpu.VMEM((sb, 640), F32), pltpu.VMEM((sb, 3 * GLA_HEADS * GLA_DK), F32)],
        compiler_params=_params(1),
        name="sample_mixers",
    )(ab, qk, vga, gqk, gv, gg, vga, sct, cft, kc, vc, s0, *const_ops)


def _lane_dims(rope_split):
    lane = np.arange(LANES)
    if rope_split:
        return (lane // 64) * 32 + lane % 32
    return lane % HEAD_DIM


def _qk_weight(w_in, rope_split):
    depth, half = w_in.shape[0], HEAD_DIM // 2
    wq = w_in[:, :, OFF_Q:OFF_K].reshape(depth, D_MODEL, N_KV_HEADS, GROUP, 2, half)
    wk = w_in[:, :, OFF_K:OFF_V].reshape(depth, D_MODEL, N_KV_HEADS, 2, half)
    if rope_split:
        wq = wq.transpose(0, 1, 3, 4, 2, 5)
        wk = wk.transpose(0, 1, 3, 2, 4)
    else:
        wq = wq.transpose(0, 1, 3, 2, 4, 5)
    return jnp.concatenate([wq.reshape(depth, D_MODEL, N_HEADS * HEAD_DIM),
                            wk.reshape(depth, D_MODEL, N_KV_HEADS * HEAD_DIM)], axis=2)


def _rope_tables(pos, dims):
    half = HEAD_DIM // 2
    inv = ROPE_THETA ** (-jnp.arange(half, dtype=F32) / half)
    ang = pos.astype(F32)[:, None] * inv[None, :]
    cos = jnp.cos(ang)[:, dims % half]
    sin = jnp.sin(ang)[:, dims % half] * jnp.where(dims < half, -1.0, 1.0).astype(F32)[None, :]
    return cos, sin


def _rest_weight(w_in):
    pad = jnp.zeros((w_in.shape[0], D_MODEL, LANES - GLA_GATE_RANK), w_in.dtype)
    return jnp.concatenate([w_in[:, :, OFF_V:OFF_GQ], w_in[:, :, OFF_GA:OFF_GATES], pad, w_in[:, :, OFF_GQ:OFF_GA]], axis=2)


def kernel(x_prompt, x_sample, cache_swa_k, cache_swa_v, state_sconv, state_cconv, state_gla, norm_mix, w_in,
           conv_short, conv_conf, conv_conf_b, conf_ln_g, conf_ln_b, q_norm, k_norm, attn_sinks, w_gla_gate,
           b_gla_gate, gla_norm, w_branch, w_out, norm_ffn, w_ffn_in, w_ffn_out):
    batch, seq, _ = x_prompt.shape
    n_dec = x_sample.shape[0]
    depth = w_in.shape[0]
    bw = BRANCH_WIDTH

    dims_p, dims_s = _lane_dims(True), _lane_dims(False)
    cos_p, sin_p = _rope_tables(jnp.arange(seq), dims_p)
    cos_s, sin_s = _rope_tables(PAST_LEN + jnp.arange(1), dims_s)
    head_order = np.array([p + GROUP * side for p in range(GROUP) for side in range(2)])
    rest_widths = (2 * LANES, 512, 512, 512)

    yp = x_prompt.reshape(batch * seq, D_MODEL)
    ys = x_sample.reshape(n_dec, D_MODEL)
    outs = {k: [] for k in ("sc_p", "cf_p", "k_p", "v_p", "S_p", "u_s", "g_s", "k_s", "v_s", "S_s")}

    rows3 = lambda a: a.reshape(depth, 1, -1)
    w_ab = w_in[:, :, :W_AB].astype(BF16)
    w_qk_p = _qk_weight(w_in, True).astype(BF16)
    w_qk_s = _qk_weight(w_in, False).astype(BF16)
    w_rest = _rest_weight(w_in).astype(BF16)
    w_gates = (w_in[:, :, OFF_GATES:].reshape(depth, D_MODEL, N_BRANCH * MERGE_CHUNKS, MERGE_COLS)
               .transpose(0, 2, 1, 3).astype(BF16))
    wbr_c = (w_branch[:, 2].reshape(depth, N_KV_HEADS, GROUP, HEAD_DIM, D_MODEL).transpose(0, 2, 1, 3, 4)
             .reshape(depth, 1, bw, D_MODEL))
    wbr = (jnp.concatenate([w_branch[:, :2], wbr_c, w_branch[:, 3:]], axis=1)
           .reshape(depth, N_BRANCH, bw, MERGE_CHUNKS, MERGE_COLS).transpose(0, 1, 3, 2, 4)
           .reshape(depth, N_BRANCH * MERGE_CHUNKS, bw, MERGE_COLS).astype(BF16))
    wout = w_out.astype(BF16)
    wfi = w_ffn_in.astype(BF16)
    wfo = w_ffn_out.astype(BF16)
    wg = jnp.concatenate([w_gla_gate, jnp.zeros((depth, LANES - GLA_GATE_RANK, GLA_HEADS * GLA_DK), F32)], axis=1).astype(BF16)
    stacked = dict(
        w_ab=w_ab, w_qk_p=w_qk_p, w_qk_s=w_qk_s, w_rest=w_rest, w_gates=w_gates, wbr=wbr, wout=wout, wfi=wfi, wfo=wfo, wg=wg,
        bg=rows3(b_gla_gate), gn=rows3(gla_norm), gm=rows3(norm_mix), gf=rows3(norm_ffn),
        qn_p=rows3(q_norm[:, dims_p]), kn_p=rows3(k_norm[:, dims_p]), qn_s=rows3(q_norm[:, dims_s]), kn_s=rows3(k_norm[:, dims_s]),
        sinks_s=jnp.broadcast_to(attn_sinks[:, :, None], (depth, N_HEADS, LANES)),
        sinks_p=jnp.broadcast_to(attn_sinks[:, head_order][:, :, None, None], (depth, N_HEADS, 1, LANES)),
        cs=conv_short, ccb=rows3(conv_conf_b), lng=rows3(conf_ln_g), lnb=rows3(conf_ln_b),
        cc=jnp.broadcast_to(conv_conf[:, :, None, :], (depth, CF_CONV_W, SUBLANES, bw)),
    )
    sct = state_sconv.transpose(0, 2, 1, 3)
    cft = state_cconv.transpose(0, 2, 1, 3)
    kc_s = cache_swa_k.reshape(depth, n_dec, WINDOW, LANES)
    vc_s = cache_swa_v.reshape(depth, n_dec, WINDOW, LANES)
    s0 = state_gla.reshape(depth, n_dec, GLA_HEADS * GLA_DK, GLA_DV)

    for l in range(depth):
        p = {name: _Layer(arr, l) for name, arr in stacked.items()}

        qk, vga, gqk, gv, gg = _inproj(yp, p["gm"], (p["w_qk_p"], p["w_rest"]), ((640,), rest_widths), TM_INPROJ)
        oc, kc, vc = _swa_prompt(qk, vga, cos_p, sin_p, p["qn_p"], p["kn_p"], p["sinks_p"], batch, seq, SWA_BLOCKS)
        od, st = _gla_prompt(gqk, gv, gg, vga, p["wg"], p["bg"], p["gn"], batch, seq, TG_GLA)
        x1, sc_new, cf_new = _mix_prompt(yp, oc, od, p["gm"], p["w_ab"], p["cs"], p["cc"], p["ccb"], p["lng"], p["lnb"],
                                         p["w_gates"], p["wbr"], p["wout"], batch, seq, TM_MIX)
        yp = _ffn(x1, p["gf"], p["wfi"], p["wfo"], TM_FFN)
        outs["sc_p"].append(sc_new)
        outs["cf_p"].append(cf_new)
        outs["k_p"].append(kc)
        outs["v_p"].append(vc)
        outs["S_p"].append(st)

        ab_s, qk_s, vga_s, gqk_s, gv_s, gg_s = _inproj(
            ys, p["gm"], (p["w_ab"], p["w_qk_s"], p["w_rest"]), ((W_AB,), (640,), rest_widths), n_dec)
        small = (p["cs"], p["cc"], p["ccb"], p["lng"], p["lnb"], p["qn_s"], p["kn_s"], cos_s, sin_s, p["sinks_s"],
                 p["wg"], p["bg"], p["gn"])
        oa_s, ob_s, oc_s, od_s, u_s, g_s, kh_s, s_new = _sample_mixers(
            ab_s, qk_s, vga_s, gqk_s, gv_s, gg_s, sct, cft, kc_s, vc_s, s0, small, SB_SAMPLE, l)
        ys = _merge(ys, oa_s, ob_s, oc_s, od_s, p["gm"], p["w_gates"], p["wbr"], p["wout"], n_dec)
        ys = _ffn(ys, p["gf"], p["wfi"], p["wfo"], n_dec)
        outs["u_s"].append(u_s)
        outs["g_s"].append(g_s)
        outs["k_s"].append(kh_s)
        outs["v_s"].append(vga_s[:, :LANES])
        outs["S_s"].append(s_new)

    st = {k: jnp.stack(v) for k, v in outs.items()}
    sc_p = st["sc_p"].reshape(depth, batch, U_PAD, bw)[:, :, U_PAD - (SC_CONV_W - 1):]
    cf_p = st["cf_p"].reshape(depth, batch, G_PAD, bw)[:, :, G_PAD - (CF_CONV_W - 1):]
    k_p = (st["k_p"].reshape(depth, batch, WINDOW, 2, N_KV_HEADS, HEAD_DIM // 2).transpose(0, 1, 2, 4, 3, 5)
           .reshape(depth, batch, WINDOW, N_KV_HEADS, HEAD_DIM))
    v_p = st["v_p"].reshape(depth, batch, WINDOW, N_KV_HEADS, HEAD_DIM)
    sp = st["S_p"].reshape(depth, batch, GLA_HEADS // 2, 2, GLA_DV, 2, GLA_DK)
    S_p = (jnp.stack([sp[:, :, :, 0, :, 0], sp[:, :, :, 1, :, 1]], axis=3)
           .reshape(depth, batch, GLA_HEADS, GLA_DV, GLA_DK).transpose(0, 1, 2, 4, 3))
    sc_s = jnp.concatenate([state_sconv[:, :, 1:], st["u_s"][:, :, None]], axis=2)
    cf_s = jnp.concatenate([state_cconv[:, :, 1:], st["g_s"][:, :, None]], axis=2)
    k_s = jnp.concatenate([cache_swa_k[:, :, 1:], st["k_s"].reshape(depth, n_dec, 1, N_KV_HEADS, HEAD_DIM)], axis=2)
    v_s = jnp.concatenate([cache_swa_v[:, :, 1:], st["v_s"].reshape(depth, n_dec, 1, N_KV_HEADS, HEAD_DIM)], axis=2)
    S_s = st["S_s"].reshape(depth, n_dec, GLA_HEADS, GLA_DK, GLA_DV)
    return (yp.reshape(batch, seq, D_MODEL), ys.reshape(n_dec, 1, D_MODEL),
            sc_p, sc_s, cf_p, cf_s, k_p, k_s, v_p, v_s, S_p, S_s)
```

```python
import functools

import numpy as np
import jax
import jax.numpy as jnp
from jax import lax
from jax.experimental import pallas as pl
from jax.experimental.pallas import tpu as pltpu

F32 = jnp.float32
BF16 = jnp.bfloat16

D_MODEL = 1024
PAST_LEN = 16384
N_BRANCH = 4
BRANCH_WIDTH = D_MODEL // 2
SC_CONV_W = 3
CF_CONV_W = 31
N_HEADS = 8
N_KV_HEADS = 2
HEAD_DIM = 64
GROUP = N_HEADS // N_KV_HEADS
WINDOW = 128
ATTN_BLOCK = 128
ROPE_THETA = 10000.0
GLA_HEADS = 4
GLA_DK = 64
GLA_DV = BRANCH_WIDTH // GLA_HEADS
GLA_GATE_RANK = 16
GLA_GATE_NORM = 16.0
GLA_CHUNK = 64
D_FF = -(-8 * D_MODEL // (3 * 256)) * 256
NEG_INF = -1e30
RMS_EPS = 1e-6
LN_EPS = 1e-5

LANES = 128
SUBLANES = 8
FF_CHUNK = 256
VMEM_LIMIT = 56 * 1024 * 1024

TM_INPROJ = 1024
SWA_BLOCKS = 8
TG_GLA = 512
TM_MIX = 512
TM_FFN = 1024
SB_SAMPLE = 8

OFF_AB = 0
W_AB = 5 * BRANCH_WIDTH
OFF_Q = W_AB
OFF_K = OFF_Q + N_HEADS * HEAD_DIM
OFF_V = OFF_K + N_KV_HEADS * HEAD_DIM
OFF_GQ = OFF_V + N_KV_HEADS * HEAD_DIM
OFF_GK = OFF_GQ + GLA_HEADS * GLA_DK
OFF_GV = OFF_GK + GLA_HEADS * GLA_DK
OFF_GG = OFF_GV + GLA_HEADS * GLA_DV
OFF_GA = OFF_GG + GLA_HEADS * GLA_DV
OFF_GATES = OFF_GA + GLA_GATE_RANK


class _Layer:
    def __init__(self, stacked, layer):
        self.stacked, self.layer = stacked, layer


def _consts(params):
    specs, operands = [], []
    for p in params:
        if isinstance(p, _Layer):
            shape, layer = p.stacked.shape[1:], p.layer
            specs.append(pl.BlockSpec((None,) + shape, lambda *_, layer=layer, nd=len(shape): (layer,) + (0,) * nd,
                                      pipeline_mode=pl.Buffered(1)))
            operands.append(p.stacked)
        else:
            specs.append(pl.BlockSpec(p.shape, lambda *_, nd=p.ndim: (0,) * nd, pipeline_mode=pl.Buffered(1)))
            operands.append(p)
    return specs, operands


def _params(n_axes, flags=None):
    return pltpu.CompilerParams(dimension_semantics=("arbitrary",) * n_axes, vmem_limit_bytes=VMEM_LIMIT, flags=flags)


def _rms(x, g):
    return x * lax.rsqrt(jnp.mean(x * x, axis=-1, keepdims=True) + RMS_EPS) * g


def _dot(a, b):
    return jnp.dot(a, b, preferred_element_type=F32)


def _dot_nt(a, b):
    return lax.dot_general(a, b, (((1,), (1,)), ((), ())), preferred_element_type=F32)


def _dot_tn(a, b):
    return lax.dot_general(a, b, (((0,), (0,)), ((), ())), preferred_element_type=F32)


def _log_sigmoid(z):
    return -(jnp.maximum(-z, 0.0) + jnp.log1p(jnp.exp(-jnp.abs(z))))


def _silu(x):
    return x * jax.nn.sigmoid(x)


def _inproj_body(x_ref, g_ref, *refs, cols):
    w_refs, out_refs = refs[:len(cols)], list(refs[len(cols):])
    xn = _rms(x_ref[...], g_ref[...]).astype(BF16)
    for w_ref, outputs in zip(w_refs, cols):
        for ranges in outputs:
            parts = [w_ref[:, off:off + wd] for off, wd in ranges]
            w = parts[0] if len(parts) == 1 else jnp.concatenate(parts, axis=1)
            out_refs.pop(0)[...] = _dot(xn, w)


def _inproj(x2d, g, weights, cols, tm):
    n = x2d.shape[0]
    flat = [sum(wd for _, wd in ranges) for outputs in cols for ranges in outputs]
    const_specs, const_ops = _consts((g,) + tuple(weights))
    return pl.pallas_call(
        functools.partial(_inproj_body, cols=cols),
        grid=(n // tm,),
        in_specs=[pl.BlockSpec((tm, D_MODEL), lambda i: (i, 0))] + const_specs,
        out_specs=[pl.BlockSpec((tm, wd), lambda i: (i, 0)) for wd in flat],
        out_shape=[jax.ShapeDtypeStruct((n, wd), F32) for wd in flat],
        compiler_params=_params(1),
        name="inproj",
    )(x2d, *const_ops)


def _swa_body(qk_ref, v_ref, cos_ref, sin_ref, qn_ref, kn_ref, sink_ref, oc_ref, kc_ref, vc_ref, kprev, vprev, *, nblk):
    j = pl.program_id(1)
    tq = ATTN_BLOCK

    @pl.when(j == 0)
    def _():
        kprev[...] = jnp.zeros_like(kprev)
        vprev[...] = jnp.zeros_like(vprev)

    blk = qk_ref[...]
    cos = cos_ref[...]
    sin = sin_ref[...]
    lane = lax.broadcasted_iota(jnp.int32, (1, LANES), 1)
    left = (lane // 32) % 2 == 0
    lo64 = lane < 64

    def norm_rope(x, g):
        x2 = x * x
        ssl = jnp.sum(jnp.where(left, x2, 0.0), axis=-1, keepdims=True)
        ssr = jnp.sum(jnp.where(left, 0.0, x2), axis=-1, keepdims=True)
        r = jnp.where(left, lax.rsqrt(ssl * (1.0 / HEAD_DIM) + RMS_EPS), lax.rsqrt(ssr * (1.0 / HEAD_DIM) + RMS_EPS))
        y = x * r * g
        return y * cos + pltpu.roll(y, 64, 1) * sin

    khat = norm_rope(blk[:, 512:640], kn_ref[...])
    v = v_ref[...]
    k_all = jnp.concatenate([kprev[...], khat], axis=0).astype(BF16)
    v_all = jnp.concatenate([vprev[...], v], axis=0).astype(BF16)
    qhat = [norm_rope(blk[:, p * LANES:(p + 1) * LANES], qn_ref[...]) * (HEAD_DIM ** -0.5) for p in range(GROUP)]
    upper = lax.broadcasted_iota(jnp.int32, (tq, tq), 1) > lax.broadcasted_iota(jnp.int32, (tq, tq), 0)
    sink = sink_ref[...][:, :, 0:1]

    for i in range(nblk):
        rows = slice(i * tq, (i + 1) * tq)
        qs = []
        for p in range(GROUP):
            qs.append(jnp.where(left, qhat[p][rows], 0.0))
            qs.append(jnp.where(left, 0.0, qhat[p][rows]))
        q_all = jnp.concatenate(qs, axis=0).astype(BF16)
        s2 = _dot_nt(q_all, k_all[i * tq:(i + 2) * tq]).reshape(N_HEADS, tq, 2 * tq)
        s_prev = s2[:, :, :tq]
        if i == 0:
            s_prev = s_prev + jnp.where(j > 0, 0.0, NEG_INF)
        s = jnp.where(upper, s_prev, s2[:, :, tq:])
        m = jnp.maximum(jnp.max(s, axis=-1, keepdims=True), sink)
        e = jnp.exp(s - m)
        inv = 1.0 / (jnp.sum(e, axis=-1, keepdims=True) + jnp.exp(sink - m))
        e2 = jnp.concatenate([jnp.where(upper, e, 0.0), jnp.where(upper, 0.0, e)], axis=-1).astype(BF16)
        o = _dot(e2.reshape(N_HEADS * tq, 2 * tq), v_all[i * tq:(i + 2) * tq]).reshape(N_HEADS, tq, LANES) * inv
        for p in range(GROUP):
            oc_ref[rows, p * LANES:(p + 1) * LANES] = jnp.where(lo64, o[2 * p], o[2 * p + 1]).astype(oc_ref.dtype)

    last = slice((nblk - 1) * tq, nblk * tq)
    kprev[...] = khat[last]
    vprev[...] = v[last]
    kc_ref[...] = khat[last]
    vc_ref[...] = v[last]


def _swa_prompt(qk, vga, cos, sin, qn, kn, sinks, batch, seq, nblk):
    ts = nblk * ATTN_BLOCK
    nb = seq // ts
    row = lambda b, j: (b * nb + j, 0)
    const_specs, const_ops = _consts((qn, kn, sinks))
    return pl.pallas_call(
        functools.partial(_swa_body, nblk=nblk),
        grid=(batch, nb),
        in_specs=[pl.BlockSpec((ts, 640), row), pl.BlockSpec((ts, LANES), row),
                  pl.BlockSpec((ts, LANES), lambda b, j: (j, 0)),
                  pl.BlockSpec((ts, LANES), lambda b, j: (j, 0))] + const_specs,
        out_specs=[pl.BlockSpec((ts, 512), row),
                   pl.BlockSpec((ATTN_BLOCK, LANES), lambda b, j: (b, 0)),
                   pl.BlockSpec((ATTN_BLOCK, LANES), lambda b, j: (b, 0))],
        out_shape=[jax.ShapeDtypeStruct((batch * seq, 512), BF16),
                   jax.ShapeDtypeStruct((batch * WINDOW, LANES), F32),
                   jax.ShapeDtypeStruct((batch * WINDOW, LANES), F32)],
        scratch_shapes=[pltpu.VMEM((ATTN_BLOCK, LANES), F32), pltpu.VMEM((ATTN_BLOCK, LANES), F32)],
        compiler_params=_params(2),
        name="swa_prompt",
    )(qk, vga, cos, sin, *const_ops)


def _split2(x):
    hi = x.astype(BF16)
    lo = (x - hi.astype(F32)).astype(BF16)
    return hi, lo


def _gla_body(gqk_ref, gv_ref, gg_ref, ga_ref, wg_ref, bg_ref, gn_ref, od_ref, st_ref, st_scr, *, tg):
    t = pl.program_id(1)
    ck = GLA_CHUNK

    @pl.when(t == 0)
    def _():
        st_scr[...] = jnp.zeros_like(st_scr)

    z = _dot(ga_ref[...].astype(BF16), wg_ref[...]) + bg_ref[...]
    la = _log_sigmoid(z) * (1.0 / GLA_GATE_NORM)
    r_i = lax.broadcasted_iota(jnp.int32, (tg, tg), 0)
    c_i = lax.broadcasted_iota(jnp.int32, (tg, tg), 1)
    same = (r_i // ck) == (c_i // ck)
    tri = jnp.where(same & (c_i <= r_i), 1.0, 0.0).astype(BF16)
    ones = jnp.where(same, 1.0, 0.0).astype(BF16)
    la_hi, la_lo = _split2(la)
    bcum = _dot(tri, la_hi) + _dot(tri, la_lo)
    btot = _dot(ones, la_hi) + _dot(ones, la_lo)

    gqk = gqk_ref[...]
    nk = GLA_HEADS * GLA_DK
    gk = gqk[:, nk:]
    qe = (gqk[:, :nk] * (GLA_DK ** -0.5)) * jnp.exp(bcum)
    ke = gk * jnp.exp(-bcum)
    kd = gk * jnp.exp(btot - bcum)
    dec = jnp.exp(btot)

    lane = lax.broadcasted_iota(jnp.int32, (1, LANES), 1)
    lo64 = lane < 64
    causal = lax.broadcasted_iota(jnp.int32, (ck, ck), 1) <= lax.broadcasted_iota(jnp.int32, (ck, ck), 0)
    u_row = lax.broadcasted_iota(jnp.int32, (2 * GLA_DV, LANES), 0)
    u_lane = lax.broadcasted_iota(jnp.int32, (2 * GLA_DV, LANES), 1)
    diag_blocks = (u_row < GLA_DV) == (u_lane < GLA_DK)
    gn = gn_ref[...]

    for c in range(tg // ck):
        rows = slice(c * ck, (c + 1) * ck)
        for p in range(GLA_HEADS // 2):
            lanes = slice(p * LANES, (p + 1) * LANES)
            vcols = slice(p * 2 * GLA_DV, (p + 1) * 2 * GLA_DV)
            qe_p = qe[rows, lanes]
            ke_p = ke[rows, lanes].astype(BF16)
            kd_p = kd[rows, lanes].astype(BF16)
            v_p = gv_ref[rows, vcols].astype(BF16)
            st = st_scr[p]
            o_halves = []
            for side in range(2):
                qm = jnp.where(lo64 == (side == 0), qe_p, 0.0).astype(BF16)
                a = jnp.where(causal, _dot_nt(qm, ke_p), 0.0).astype(BF16)
                o_halves.append(_dot(a, v_p[:, side * GLA_DV:(side + 1) * GLA_DV]))
            o = jnp.concatenate(o_halves, axis=1) + _dot_nt(qe_p.astype(BF16), st.astype(BF16))
            upd = jnp.where(diag_blocks, _dot_tn(v_p, kd_p), 0.0)
            st_scr[p] = dec[c * ck:c * ck + 1, lanes] * st + upd
            gate = gg_ref[rows, vcols]
            for side in range(2):
                hs = slice(side * GLA_DV, (side + 1) * GLA_DV)
                od_ref[rows, p * 2 * GLA_DV + side * GLA_DV:p * 2 * GLA_DV + (side + 1) * GLA_DV] = (
                    _rms(o[:, hs], gn) * _silu(gate[:, hs])).astype(od_ref.dtype)

    st_ref[...] = st_scr[...]


def _gla_prompt(gqk, gv, gg, vga, wg, bg, gn, batch, seq, tg):
    nt = seq // tg
    row = lambda b, t: (b * nt + t, 0)
    ga_row = lambda b, t: (b * nt + t, 1)
    const_specs, const_ops = _consts((wg, bg, gn))
    return pl.pallas_call(
        functools.partial(_gla_body, tg=tg),
        grid=(batch, nt),
        in_specs=[pl.BlockSpec((tg, 512), row), pl.BlockSpec((tg, 512), row), pl.BlockSpec((tg, 512), row),
                  pl.BlockSpec((tg, LANES), ga_row)] + const_specs,
        out_specs=[pl.BlockSpec((tg, 512), row),
                   pl.BlockSpec((None, 2, 2 * GLA_DV, LANES), lambda b, t: (b, 0, 0, 0))],
        out_shape=[jax.ShapeDtypeStruct((batch * seq, 512), BF16),
                   jax.ShapeDtypeStruct((batch, 2, 2 * GLA_DV, LANES), F32)],
        scratch_shapes=[pltpu.VMEM((2, 2 * GLA_DV, LANES), F32)],
        compiler_params=_params(2),
        name="gla_prompt",
    )(gqk, gv, gg, vga, *const_ops)


U_PAD = 8
G_PAD = 32
CONV_ROWS = 16


def _conv_prepare(xn, w_ref, cs_ref, ubuf, gbuf, gsh, oa_scr, tm):
    i = pl.program_id(1)
    bw = BRANCH_WIDTH

    @pl.when(i == 0)
    def _():
        ubuf[0:U_PAD, :] = jnp.zeros((U_PAD, bw), F32)
        gbuf[0:G_PAD, :] = jnp.zeros((G_PAD, bw), F32)

    @pl.when(i > 0)
    def _():
        ubuf[0:U_PAD, :] = ubuf[tm:tm + U_PAD, :]
        gbuf[0:G_PAD, :] = gbuf[tm:tm + G_PAD, :]

    sc_h = _dot(xn, w_ref[:, 0:bw])
    sc_c = _dot(xn, w_ref[:, 2 * bw:3 * bw])
    u = sc_c * sc_h
    ubuf[U_PAD:U_PAD + tm, :] = u
    conv_u = (cs_ref[0:1, :] * ubuf[U_PAD - 2:U_PAD - 2 + tm, :] + cs_ref[1:2, :] * ubuf[U_PAD - 1:U_PAD - 1 + tm, :]
              + cs_ref[2:3, :] * u)
    oa_scr[...] = (_dot(xn, w_ref[:, bw:2 * bw]) * conv_u).astype(oa_scr.dtype)

    cf_a = _dot(xn, w_ref[:, 3 * bw:4 * bw])
    cf_b = _dot(xn, w_ref[:, 4 * bw:5 * bw])
    gbuf[G_PAD:G_PAD + tm, :] = cf_a * jax.nn.sigmoid(cf_b)
    span = tm + G_PAD - SUBLANES
    for sh in range(1, SUBLANES):
        gsh[sh - 1] = gbuf[sh:sh + span, :]


def _conv_block(r0, cc_ref, gbuf, gsh, cbuf):
    base = G_PAD - (CF_CONV_W - 1)
    acc = None
    for k in range(CF_CONV_W):
        al, sh = divmod(base + k, SUBLANES)
        rows = pl.ds(pl.multiple_of(r0 + al * SUBLANES, SUBLANES), CONV_ROWS)
        w_tap = jnp.concatenate([cc_ref[k]] * (CONV_ROWS // SUBLANES), axis=0)
        tap = w_tap * (gbuf[rows, :] if sh == 0 else gsh[sh - 1, rows, :])
        acc = tap if acc is None else acc + tap
    cbuf[pl.ds(pl.multiple_of(r0, CONV_ROWS), CONV_ROWS), :] = acc


def _conv_finish(cbuf, ccb_ref, lng_ref, lnb_ref):
    c = cbuf[...] + ccb_ref[...]
    mu = jnp.mean(c, axis=-1, keepdims=True)
    xc = c - mu
    y = xc * lax.rsqrt(jnp.mean(xc * xc, axis=-1, keepdims=True) + LN_EPS) * lng_ref[...] + lnb_ref[...]
    return _silu(y)


MERGE_COLS = 256
MERGE_CHUNKS = D_MODEL // MERGE_COLS


def _merge_term(xn, o, r, c, wgt_ref, wbr_ref):
    idx = r * MERGE_CHUNKS + c
    return jax.nn.sigmoid(_dot(xn, wgt_ref[idx])) * _dot(o, wbr_ref[idx])


def _merge_out(x, xn, branches, wgt_ref, wbr_ref, wout_ref):
    y = x
    for c in range(MERGE_CHUNKS):
        merged = None
        for r, o in enumerate(branches):
            term = _merge_term(xn, o.astype(BF16), r, c, wgt_ref, wbr_ref)
            merged = term if merged is None else merged + term
        y = y + _dot(merged.astype(BF16), wout_ref[c * MERGE_COLS:(c + 1) * MERGE_COLS, :])
    return y


def _swiglu(x1, gf_ref, wfi_ref, wfo_ref):
    xn2 = _rms(x1, gf_ref[...]).astype(BF16)
    y = x1
    for c0 in range(0, D_FF, FF_CHUNK):
        h_gate = _dot(xn2, wfi_ref[:, c0:c0 + FF_CHUNK])
        h_up = _dot(xn2, wfi_ref[:, D_FF + c0:D_FF + c0 + FF_CHUNK])
        y = y + _dot((_silu(h_gate) * h_up).astype(BF16), wfo_ref[c0:c0 + FF_CHUNK, :])
    return y


def _mix_body(x_ref, oc_ref, od_ref, gm_ref, wab_ref, cs_ref, cc_ref, ccb_ref, lng_ref, lnb_ref,
              wgt_ref, wbr_ref, wout_ref, x1_ref, sc_ref, cf_ref, ubuf, gbuf, gsh, cbuf, xn_scr, oa_scr, macc, *, tm):
    xn_scr[...] = _rms(x_ref[...], gm_ref[...]).astype(BF16)
    _conv_prepare(xn_scr[...], wab_ref, cs_ref, ubuf, gbuf, gsh, oa_scr, tm)

    rows_per_step = tm // MERGE_CHUNKS

    def step(c, carry):
        xn = xn_scr[...]
        macc[c] = (_merge_term(xn, oc_ref[...], 2, c, wgt_ref, wbr_ref) + _merge_term(xn, od_ref[...], 3, c, wgt_ref, wbr_ref)
                   + _merge_term(xn, oa_scr[...], 0, c, wgt_ref, wbr_ref))
        row0 = c * rows_per_step
        for r0 in range(0, rows_per_step, CONV_ROWS):
            _conv_block(row0 + r0, cc_ref, gbuf, gsh, cbuf)
        return carry

    lax.fori_loop(0, MERGE_CHUNKS, step, 0, unroll=2)

    o_b = _conv_finish(cbuf, ccb_ref, lng_ref, lnb_ref).astype(BF16)
    xn = xn_scr[...]
    y = x_ref[...]
    for c in range(MERGE_CHUNKS):
        merged = macc[c] + _merge_term(xn, o_b, 1, c, wgt_ref, wbr_ref)
        y = y + _dot(merged.astype(BF16), wout_ref[c * MERGE_COLS:(c + 1) * MERGE_COLS, :])
    x1_ref[...] = y
    sc_ref[...] = ubuf[tm:tm + U_PAD, :]
    cf_ref[...] = gbuf[tm:tm + G_PAD, :]


def _mix_prompt(x2d, oc, od, gm, w_ab, cs, cc, ccb, lng, lnb, wgt, wbr, wout, batch, seq, tm):
    nt = seq // tm
    row = lambda b, i: (b * nt + i, 0)
    bw = BRANCH_WIDTH
    const_specs, const_ops = _consts((gm, w_ab, cs, cc, ccb, lng, lnb, wgt, wbr, wout))
    return pl.pallas_call(
        functools.partial(_mix_body, tm=tm),
        grid=(batch, nt),
        in_specs=[pl.BlockSpec((tm, D_MODEL), row), pl.BlockSpec((tm, bw), row), pl.BlockSpec((tm, bw), row)]
                 + const_specs,
        out_specs=[pl.BlockSpec((tm, D_MODEL), row),
                   pl.BlockSpec((U_PAD, bw), lambda b, i: (b, 0)), pl.BlockSpec((G_PAD, bw), lambda b, i: (b, 0))],
        out_shape=[jax.ShapeDtypeStruct((batch * seq, D_MODEL), F32),
                   jax.ShapeDtypeStruct((batch * U_PAD, bw), F32), jax.ShapeDtypeStruct((batch * G_PAD, bw), F32)],
        scratch_shapes=[pltpu.VMEM((U_PAD + tm, bw), F32), pltpu.VMEM((G_PAD + tm, bw), F32),
                        pltpu.VMEM((SUBLANES - 1, tm + G_PAD - SUBLANES, bw), F32), pltpu.VMEM((tm, bw), F32),
                        pltpu.VMEM((tm, D_MODEL), BF16), pltpu.VMEM((tm, bw), BF16),
                        pltpu.VMEM((MERGE_CHUNKS, tm, MERGE_COLS), F32)],
        compiler_params=_params(2),
        name="mix_prompt",
    )(x2d, oc, od, *const_ops)


def _ffn_body(x1_ref, gf_ref, wfi_ref, wfo_ref, y_ref):
    y_ref[...] = _swiglu(x1_ref[...], gf_ref, wfi_ref, wfo_ref)


def _ffn(x2d, gf, wfi, wfo, tm):
    n = x2d.shape[0]
    row = lambda i: (i, 0)
    const_specs, const_ops = _consts((gf, wfi, wfo))
    return pl.pallas_call(
        _ffn_body,
        grid=(n // tm,),
        in_specs=[pl.BlockSpec((tm, D_MODEL), row)] + const_specs,
        out_specs=pl.BlockSpec((tm, D_MODEL), row),
        out_shape=jax.ShapeDtypeStruct((n, D_MODEL), F32),
        compiler_params=_params(1),
        name="ffn",
    )(x2d, *const_ops)


def _merge_body(x_ref, oa_ref, ob_ref, oc_ref, od_ref, gm_ref, wgt_ref, wbr_ref, wout_ref, x1_ref):
    x = x_ref[...]
    xn = _rms(x, gm_ref[...]).astype(BF16)
    x1_ref[...] = _merge_out(x, xn, (oa_ref[...], ob_ref[...], oc_ref[...], od_ref[...]), wgt_ref, wbr_ref, wout_ref)


def _merge(x2d, oa, ob, oc, od, gm, wgt, wbr, wout, tm):
    n = x2d.shape[0]
    bw = BRANCH_WIDTH
    row = lambda i: (i, 0)
    const_specs, const_ops = _consts((gm, wgt, wbr, wout))
    return pl.pallas_call(
        _merge_body,
        grid=(n // tm,),
        in_specs=[pl.BlockSpec((tm, D_MODEL), row)] + [pl.BlockSpec((tm, bw), row)] * 4 + const_specs,
        out_specs=pl.BlockSpec((tm, D_MODEL), row),
        out_shape=jax.ShapeDtypeStruct((n, D_MODEL), F32),
        compiler_params=_params(1),
        name="merge",
    )(x2d, oa, ob, oc, od, *const_ops)


def _sample_body(ab_ref, qk_ref, v_ref, gqk_ref, gv_ref, gg_ref, ga_ref, sct_ref, cft_ref, kc_ref, vc_ref, s_ref,
                 cs_ref, cc_ref, ccb_ref, lng_ref, lnb_ref, qn_ref, kn_ref, cos_ref, sin_ref, sink_ref,
                 wg_ref, bg_ref, gn_ref,
                 oa_ref, ob_ref, oc_ref, od_ref, u_ref, g_ref, kh_ref, sn_ref,
                 qh_scr, gl_scr, *, sb):
    bw = BRANCH_WIDTH
    sc_h = ab_ref[:, 0:bw]
    sc_b = ab_ref[:, bw:2 * bw]
    sc_c = ab_ref[:, 2 * bw:3 * bw]
    u = sc_c * sc_h
    u_ref[...] = u
    oa_ref[...] = sc_b * (cs_ref[0:1, :] * sct_ref[0] + cs_ref[1:2, :] * sct_ref[1] + cs_ref[2:3, :] * u)
    g = ab_ref[:, 3 * bw:4 * bw] * jax.nn.sigmoid(ab_ref[:, 4 * bw:5 * bw])
    g_ref[...] = g
    c = ccb_ref[...] + cc_ref[CF_CONV_W - 1, 0:1, :] * g
    for k in range(CF_CONV_W - 1):
        c = c + cc_ref[k, 0:1, :] * cft_ref[k]
    mu = jnp.mean(c, axis=-1, keepdims=True)
    xc = c - mu
    ob_ref[...] = _silu(xc * lax.rsqrt(jnp.mean(xc * xc, axis=-1, keepdims=True) + LN_EPS) * lng_ref[...] + lnb_ref[...])

    lane = lax.broadcasted_iota(jnp.int32, (1, LANES), 1)
    lo64 = lane < 64
    first32 = (lane % 64) < 32
    cos = cos_ref[...]
    sin = sin_ref[...]

    def norm_rope(x, gw):
        x2 = x * x
        ssl = jnp.sum(jnp.where(lo64, x2, 0.0), axis=-1, keepdims=True)
        ssr = jnp.sum(jnp.where(lo64, 0.0, x2), axis=-1, keepdims=True)
        r = jnp.where(lo64, lax.rsqrt(ssl * (1.0 / HEAD_DIM) + RMS_EPS), lax.rsqrt(ssr * (1.0 / HEAD_DIM) + RMS_EPS))
        y = x * r * gw
        partner = jnp.where(first32, pltpu.roll(y, LANES - 32, 1), pltpu.roll(y, 32, 1))
        return y * cos + partner * sin

    for p in range(GROUP):
        qh_scr[:, p * LANES:(p + 1) * LANES] = norm_rope(qk_ref[:, p * LANES:(p + 1) * LANES], qn_ref[...])
    khat = norm_rope(qk_ref[:, 512:640], kn_ref[...])
    kh_ref[...] = khat
    qh_scr[:, 512:640] = khat

    z = _dot(ga_ref[...].astype(BF16), wg_ref[...]) + bg_ref[...]
    nk = GLA_HEADS * GLA_DK
    gl_scr[:, 0:nk] = jnp.exp(_log_sigmoid(z) * (1.0 / GLA_GATE_NORM))
    gl_scr[:, nk:2 * nk] = gqk_ref[:, 0:nk] * (GLA_DK ** -0.5)
    gl_scr[:, 2 * nk:3 * nk] = gqk_ref[:, nk:2 * nk]

    top = lax.broadcasted_iota(jnp.int32, (LANES, LANES), 0) < GLA_DK
    tile_row = lax.broadcasted_iota(jnp.int32, (SUBLANES, LANES), 0)
    ext_row = lax.broadcasted_iota(jnp.int32, (WINDOW + SUBLANES, 1), 0)
    key_bias = jnp.where((ext_row == 0) | (ext_row > WINDOW), NEG_INF, 0.0)
    sink = sink_ref[...]
    gla_cols = [[gl_scr[:, j * nk + p * LANES:j * nk + (p + 1) * LANES].T for j in range(3)]
                for p in range(GLA_HEADS // 2)]

    for b in range(sb):
        k_new = qh_scr[b:b + 1, 512:640]
        v_new = v_ref[b:b + 1, :]
        kext = jnp.concatenate([kc_ref[b], jnp.where(tile_row == 0, k_new, 0.0)], axis=0).astype(BF16)
        vext = jnp.concatenate([vc_ref[b], jnp.where(tile_row == 0, v_new, 0.0)], axis=0).astype(BF16)
        q_rows = jnp.zeros((SUBLANES, LANES), F32)
        for p in range(GROUP):
            q_row = qh_scr[b:b + 1, p * LANES:(p + 1) * LANES]
            q_rows = jnp.where(tile_row == 2 * p, jnp.where(lo64, q_row, 0.0), q_rows)
            q_rows = jnp.where(tile_row == 2 * p + 1, jnp.where(lo64, 0.0, q_row), q_rows)
        s = _dot_nt(kext, q_rows.astype(BF16)) * (HEAD_DIM ** -0.5) + key_bias
        m = jnp.maximum(jnp.max(s, axis=0, keepdims=True), sink)
        e = jnp.exp(s - m)
        prob = (e / (jnp.sum(e, axis=0, keepdims=True) + jnp.exp(sink - m))).astype(BF16)
        o = _dot_tn(prob, vext)
        for p in range(GROUP):
            oc_ref[b:b + 1, p * LANES:(p + 1) * LANES] = jnp.where(lo64, o[2 * p:2 * p + 1], o[2 * p + 1:2 * p + 2])

        for p in range(GLA_HEADS // 2):
            lanes = slice(p * LANES, (p + 1) * LANES)
            a_col, q_col, k_col = (col[:, b:b + 1] for col in gla_cols[p])
            v0 = gv_ref[b:b + 1, 2 * p * GLA_DV:(2 * p + 1) * GLA_DV]
            v1 = gv_ref[b:b + 1, (2 * p + 1) * GLA_DV:(2 * p + 2) * GLA_DV]
            vsel = jnp.where(top, jnp.broadcast_to(v0, (LANES, GLA_DV)), jnp.broadcast_to(v1, (LANES, GLA_DV)))
            s_new = a_col * s_ref[b, lanes, :] + k_col * vsel
            sn_ref[b, lanes, :] = s_new
            w = q_col * s_new
            od_ref[b:b + 1, 2 * p * GLA_DV:(2 * p + 1) * GLA_DV] = jnp.sum(w[0:GLA_DK], axis=0, keepdims=True)
            od_ref[b:b + 1, (2 * p + 1) * GLA_DV:(2 * p + 2) * GLA_DV] = jnp.sum(w[GLA_DK:], axis=0, keepdims=True)

    gn = gn_ref[...]
    for h in range(GLA_HEADS):
        hs = slice(h * GLA_DV, (h + 1) * GLA_DV)
        od_ref[:, hs] = _rms(od_ref[:, hs], gn) * _silu(gg_ref[:, hs])


def _sample_mixers(ab, qk, vga, gqk, gv, gg, sct, cft, kc, vc, s0, small, sb, layer):
    n = ab.shape[0]
    bw = BRANCH_WIDTH
    row = lambda i: (i, 0)
    ga_row = lambda i: (i, 1)
    lead3 = lambda i: (i, 0, 0)
    state_lead = lambda i: (layer, i, 0, 0)
    state_mid = lambda i: (layer, 0, i, 0)
    const_specs, const_ops = _consts(small)
    return pl.pallas_call(
        functools.partial(_sample_body, sb=sb),
        grid=(n // sb,),
        in_specs=[pl.BlockSpec((sb, W_AB), row), pl.BlockSpec((sb, 640), row), pl.BlockSpec((sb, LANES), row),
                  pl.BlockSpec((sb, 512), row),
                  pl.BlockSpec((sb, 512), row), pl.BlockSpec((sb, 512), row), pl.BlockSpec((sb, LANES), ga_row),
                  pl.BlockSpec((None, SC_CONV_W - 1, sb, bw), state_mid),
                  pl.BlockSpec((None, CF_CONV_W - 1, sb, bw), state_mid),
                  pl.BlockSpec((None, sb, WINDOW, LANES), state_lead), pl.BlockSpec((None, sb, WINDOW, LANES), state_lead),
                  pl.BlockSpec((None, sb, GLA_HEADS * GLA_DK, GLA_DV), state_lead)] + const_specs,
        out_specs=[pl.BlockSpec((sb, bw), row)] * 6 + [pl.BlockSpec((sb, LANES), row),
                                                       pl.BlockSpec((sb, GLA_HEADS * GLA_DK, GLA_DV), lead3)],
        out_shape=[jax.ShapeDtypeStruct((n, bw), F32)] * 6 + [jax.ShapeDtypeStruct((n, LANES), F32),
                                                              jax.ShapeDtypeStruct(s0.shape[1:], F32)],
        scratch_shapes=[pltpu.VMEM((sb, 640), F32), pltpu.VMEM((sb, 3 * GLA_HEADS * GLA_DK), F32)],
        compiler_params=_params(1),
        name="sample_mixers",
    )(ab, qk, vga, gqk, gv, gg, vga, sct, cft, kc, vc, s0, *const_ops)


def _lane_dims(rope_split):
    lane = np.arange(LANES)
    if rope_split:
        return (lane // 64) * 32 + lane % 32
    return lane % HEAD_DIM


def _qk_weight(w_in, rope_split):
    depth, half = w_in.shape[0], HEAD_DIM // 2
    wq = w_in[:, :, OFF_Q:OFF_K].reshape(depth, D_MODEL, N_KV_HEADS, GROUP, 2, half)
    wk = w_in[:, :, OFF_K:OFF_V].reshape(depth, D_MODEL, N_KV_HEADS, 2, half)
    if rope_split:
        wq = wq.transpose(0, 1, 3, 4, 2, 5)
        wk = wk.transpose(0, 1, 3, 2, 4)
    else:
        wq = wq.transpose(0, 1, 3, 2, 4, 5)
    return jnp.concatenate([wq.reshape(depth, D_MODEL, N_HEADS * HEAD_DIM),
                            wk.reshape(depth, D_MODEL, N_KV_HEADS * HEAD_DIM)], axis=2)


def _rope_tables(pos, dims):
    half = HEAD_DIM // 2
    inv = ROPE_THETA ** (-jnp.arange(half, dtype=F32) / half)
    ang = pos.astype(F32)[:, None] * inv[None, :]
    cos = jnp.cos(ang)[:, dims % half]
    sin = jnp.sin(ang)[:, dims % half] * jnp.where(dims < half, -1.0, 1.0).astype(F32)[None, :]
    return cos, sin


def _rest_weight(w_in):
    pad = jnp.zeros((w_in.shape[0], D_MODEL, LANES - GLA_GATE_RANK), w_in.dtype)
    return jnp.concatenate([w_in[:, :, OFF_V:OFF_GATES], pad], axis=2)


def kernel(x_prompt, x_sample, cache_swa_k, cache_swa_v, state_sconv, state_cconv, state_gla, norm_mix, w_in,
           conv_short, conv_conf, conv_conf_b, conf_ln_g, conf_ln_b, q_norm, k_norm, attn_sinks, w_gla_gate,
           b_gla_gate, gla_norm, w_branch, w_out, norm_ffn, w_ffn_in, w_ffn_out):
    batch, seq, _ = x_prompt.shape
    n_dec = x_sample.shape[0]
    depth = w_in.shape[0]
    bw = BRANCH_WIDTH

    dims_p, dims_s = _lane_dims(True), _lane_dims(False)
    cos_p, sin_p = _rope_tables(jnp.arange(seq), dims_p)
    cos_s, sin_s = _rope_tables(PAST_LEN + jnp.arange(1), dims_s)
    head_order = np.array([p + GROUP * side for p in range(GROUP) for side in range(2)])
    rest_cols = (((0, LANES), (LANES + 3 * 512, LANES)), ((LANES, 512),), ((LANES + 512, 512),), ((LANES + 1024, 512),))
    qk_cols = (((0, 640),),)
    ab_cols = (((0, W_AB),),)

    yp = x_prompt.reshape(batch * seq, D_MODEL)
    ys = x_sample.reshape(n_dec, D_MODEL)
    outs = {k: [] for k in ("sc_p", "cf_p", "k_p", "v_p", "S_p", "u_s", "g_s", "k_s", "v_s", "S_s")}

    rows3 = lambda a: a.reshape(depth, 1, -1)
    w_ab = w_in[:, :, :W_AB].astype(BF16)
    w_qk_p = _qk_weight(w_in, True).astype(BF16)
    w_qk_s = _qk_weight(w_in, False).astype(BF16)
    w_rest = _rest_weight(w_in).astype(BF16)
    w_gates = (w_in[:, :, OFF_GATES:].reshape(depth, D_MODEL, N_BRANCH * MERGE_CHUNKS, MERGE_COLS)
               .transpose(0, 2, 1, 3).astype(BF16))
    wbr_c = (w_branch[:, 2].reshape(depth, N_KV_HEADS, GROUP, HEAD_DIM, D_MODEL).transpose(0, 2, 1, 3, 4)
             .reshape(depth, 1, bw, D_MODEL))
    wbr = (jnp.concatenate([w_branch[:, :2], wbr_c, w_branch[:, 3:]], axis=1)
           .reshape(depth, N_BRANCH, bw, MERGE_CHUNKS, MERGE_COLS).transpose(0, 1, 3, 2, 4)
           .reshape(depth, N_BRANCH * MERGE_CHUNKS, bw, MERGE_COLS).astype(BF16))
    wout = w_out.astype(BF16)
    wfi = w_ffn_in.astype(BF16)
    wfo = w_ffn_out.astype(BF16)
    wg = jnp.concatenate([w_gla_gate, jnp.zeros((depth, LANES - GLA_GATE_RANK, GLA_HEADS * GLA_DK), F32)], axis=1).astype(BF16)
    stacked = dict(
        w_ab=w_ab, w_qk_p=w_qk_p, w_qk_s=w_qk_s, w_rest=w_rest, w_gates=w_gates, wbr=wbr, wout=wout, wfi=wfi, wfo=wfo, wg=wg,
        bg=rows3(b_gla_gate), gn=rows3(gla_norm), gm=rows3(norm_mix), gf=rows3(norm_ffn),
        qn_p=rows3(q_norm[:, dims_p]), kn_p=rows3(k_norm[:, dims_p]), qn_s=rows3(q_norm[:, dims_s]), kn_s=rows3(k_norm[:, dims_s]),
        sinks_s=attn_sinks[:, head_order].reshape(depth, 1, N_HEADS),
        sinks_p=jnp.broadcast_to(attn_sinks[:, head_order][:, :, None, None], (depth, N_HEADS, 1, LANES)),
        cs=conv_short, ccb=rows3(conv_conf_b), lng=rows3(conf_ln_g), lnb=rows3(conf_ln_b),
        cc=jnp.broadcast_to(conv_conf[:, :, None, :], (depth, CF_CONV_W, SUBLANES, bw)),
    )
    sct = state_sconv.transpose(0, 2, 1, 3)
    cft = state_cconv.transpose(0, 2, 1, 3)
    kc_s = cache_swa_k.reshape(depth, n_dec, WINDOW, LANES)
    vc_s = cache_swa_v.reshape(depth, n_dec, WINDOW, LANES)
    s0 = state_gla.reshape(depth, n_dec, GLA_HEADS * GLA_DK, GLA_DV)

    for l in range(depth):
        p = {name: _Layer(arr, l) for name, arr in stacked.items()}

        qk, vga, gqk, gv, gg = _inproj(yp, p["gm"], (p["w_qk_p"], p["w_rest"]), (qk_cols, rest_cols), TM_INPROJ)
        oc, kc, vc = _swa_prompt(qk, vga, cos_p, sin_p, p["qn_p"], p["kn_p"], p["sinks_p"], batch, seq, SWA_BLOCKS)
        od, st = _gla_prompt(gqk, gv, gg, vga, p["wg"], p["bg"], p["gn"], batch, seq, TG_GLA)
        x1, sc_new, cf_new = _mix_prompt(yp, oc, od, p["gm"], p["w_ab"], p["cs"], p["cc"], p["ccb"], p["lng"], p["lnb"],
                                         p["w_gates"], p["wbr"], p["wout"], batch, seq, TM_MIX)
        yp = _ffn(x1, p["gf"], p["wfi"], p["wfo"], TM_FFN)
        outs["sc_p"].append(sc_new)
        outs["cf_p"].append(cf_new)
        outs["k_p"].append(kc)
        outs["v_p"].append(vc)
        outs["S_p"].append(st)

        ab_s, qk_s, vga_s, gqk_s, gv_s, gg_s = _inproj(
            ys, p["gm"], (p["w_ab"], p["w_qk_s"], p["w_rest"]), (ab_cols, qk_cols, rest_cols), n_dec)
        small = (p["cs"], p["cc"], p["ccb"], p["lng"], p["lnb"], p["qn_s"], p["kn_s"], cos_s, sin_s, p["sinks_s"],
                 p["wg"], p["bg"], p["gn"])
        oa_s, ob_s, oc_s, od_s, u_s, g_s, kh_s, s_new = _sample_mixers(
            ab_s, qk_s, vga_s, gqk_s, gv_s, gg_s, sct, cft, kc_s, vc_s, s0, small, SB_SAMPLE, l)
        ys = _merge(ys, oa_s, ob_s, oc_s, od_s, p["gm"], p["w_gates"], p["wbr"], p["wout"], n_dec)
        ys = _ffn(ys, p["gf"], p["wfi"], p["wfo"], n_dec)
        outs["u_s"].append(u_s)
        outs["g_s"].append(g_s)
        outs["k_s"].append(kh_s)
        outs["v_s"].append(vga_s[:, :LANES])
        outs["S_s"].append(s_new)

    st = {k: jnp.stack(v) for k, v in outs.items()}
    sc_p = st["sc_p"].reshape(depth, batch, U_PAD, bw)[:, :, U_PAD - (SC_CONV_W - 1):]
    cf_p = st["cf_p"].reshape(depth, batch, G_PAD, bw)[:, :, G_PAD - (CF_CONV_W - 1):]
    k_p = (st["k_p"].reshape(depth, batch, WINDOW, 2, N_KV_HEADS, HEAD_DIM // 2).transpose(0, 1, 2, 4, 3, 5)
           .reshape(depth, batch, WINDOW, N_KV_HEADS, HEAD_DIM))
    v_p = st["v_p"].reshape(depth, batch, WINDOW, N_KV_HEADS, HEAD_DIM)
    sp = st["S_p"].reshape(depth, batch, GLA_HEADS // 2, 2, GLA_DV, 2, GLA_DK)
    S_p = (jnp.stack([sp[:, :, :, 0, :, 0], sp[:, :, :, 1, :, 1]], axis=3)
           .reshape(depth, batch, GLA_HEADS, GLA_DV, GLA_DK).transpose(0, 1, 2, 4, 3))
    sc_s = jnp.concatenate([state_sconv[:, :, 1:], st["u_s"][:, :, None]], axis=2)
    cf_s = jnp.concatenate([state_cconv[:, :, 1:], st["g_s"][:, :, None]], axis=2)
    k_s = jnp.concatenate([cache_swa_k[:, :, 1:], st["k_s"].reshape(depth, n_dec, 1, N_KV_HEADS, HEAD_DIM)], axis=2)
    v_s = jnp.concatenate([cache_swa_v[:, :, 1:], st["v_s"].reshape(depth, n_dec, 1, N_KV_HEADS, HEAD_DIM)], axis=2)
    S_s = st["S_s"].reshape(depth, n_dec, GLA_HEADS, GLA_DK, GLA_DV)
    return (yp.reshape(batch, seq, D_MODEL), ys.reshape(n_dec, 1, D_MODEL),
            sc_p, sc_s, cf_p, cf_s, k_p, k_s, v_p, v_s, S_p, S_s)
```

```python
import functools

import numpy as np
import jax
import jax.numpy as jnp
from jax import lax
from jax.experimental import pallas as pl
from jax.experimental.pallas import tpu as pltpu

F32 = jnp.float32
BF16 = jnp.bfloat16

D_MODEL = 1024
PAST_LEN = 16384
N_BRANCH = 4
BRANCH_WIDTH = D_MODEL // 2
SC_CONV_W = 3
CF_CONV_W = 31
N_HEADS = 8
N_KV_HEADS = 2
HEAD_DIM = 64
GROUP = N_HEADS // N_KV_HEADS
WINDOW = 128
ATTN_BLOCK = 128
ROPE_THETA = 10000.0
GLA_HEADS = 4
GLA_DK = 64
GLA_DV = BRANCH_WIDTH // GLA_HEADS
GLA_GATE_RANK = 16
GLA_GATE_NORM = 16.0
GLA_CHUNK = 64
D_FF = -(-8 * D_MODEL // (3 * 256)) * 256
NEG_INF = -1e30
RMS_EPS = 1e-6
LN_EPS = 1e-5

LANES = 128
SUBLANES = 8
FF_CHUNK = 256
VMEM_LIMIT = 56 * 1024 * 1024

TM_INPROJ = 1024
SWA_BLOCKS = 8
TG_GLA = 512
TM_MIX = 512
TM_FFN = 1024
SB_SAMPLE = 8

OFF_AB = 0
W_AB = 5 * BRANCH_WIDTH
OFF_Q = W_AB
OFF_K = OFF_Q + N_HEADS * HEAD_DIM
OFF_V = OFF_K + N_KV_HEADS * HEAD_DIM
OFF_GQ = OFF_V + N_KV_HEADS * HEAD_DIM
OFF_GK = OFF_GQ + GLA_HEADS * GLA_DK
OFF_GV = OFF_GK + GLA_HEADS * GLA_DK
OFF_GG = OFF_GV + GLA_HEADS * GLA_DV
OFF_GA = OFF_GG + GLA_HEADS * GLA_DV
OFF_GATES = OFF_GA + GLA_GATE_RANK


class _Layer:
    def __init__(self, stacked, layer):
        self.stacked, self.layer = stacked, layer


def _consts(params):
    specs, operands = [], []
    for p in params:
        if isinstance(p, _Layer):
            shape, layer = p.stacked.shape[1:], p.layer
            specs.append(pl.BlockSpec((None,) + shape, lambda *_, layer=layer, nd=len(shape): (layer,) + (0,) * nd,
                                      pipeline_mode=pl.Buffered(1)))
            operands.append(p.stacked)
        else:
            specs.append(pl.BlockSpec(p.shape, lambda *_, nd=p.ndim: (0,) * nd, pipeline_mode=pl.Buffered(1)))
            operands.append(p)
    return specs, operands


def _params(n_axes, flags=None):
    return pltpu.CompilerParams(dimension_semantics=("arbitrary",) * n_axes, vmem_limit_bytes=VMEM_LIMIT, flags=flags)


def _rms(x, g):
    return x * lax.rsqrt(jnp.mean(x * x, axis=-1, keepdims=True) + RMS_EPS) * g


def _dot(a, b):
    return jnp.dot(a, b, preferred_element_type=F32)


def _dot_nt(a, b):
    return lax.dot_general(a, b, (((1,), (1,)), ((), ())), preferred_element_type=F32)


def _dot_tn(a, b):
    return lax.dot_general(a, b, (((0,), (0,)), ((), ())), preferred_element_type=F32)


def _log_sigmoid(z):
    return -(jnp.maximum(-z, 0.0) + jnp.log1p(jnp.exp(-jnp.abs(z))))


def _silu(x):
    return x * jax.nn.sigmoid(x)


def _inproj_body(x_ref, g_ref, *refs, cols):
    w_refs, out_refs = refs[:len(cols)], list(refs[len(cols):])
    xn = _rms(x_ref[...], g_ref[...]).astype(BF16)
    for w_ref, outputs in zip(w_refs, cols):
        for ranges in outputs:
            parts = [w_ref[:, off:off + wd] for off, wd in ranges]
            w = parts[0] if len(parts) == 1 else jnp.concatenate(parts, axis=1)
            out_refs.pop(0)[...] = _dot(xn, w)


def _inproj(x2d, g, weights, cols, tm):
    n = x2d.shape[0]
    flat = [sum(wd for _, wd in ranges) for outputs in cols for ranges in outputs]
    const_specs, const_ops = _consts((g,) + tuple(weights))
    return pl.pallas_call(
        functools.partial(_inproj_body, cols=cols),
        grid=(n // tm,),
        in_specs=[pl.BlockSpec((tm, D_MODEL), lambda i: (i, 0))] + const_specs,
        out_specs=[pl.BlockSpec((tm, wd), lambda i: (i, 0)) for wd in flat],
        out_shape=[jax.ShapeDtypeStruct((n, wd), F32) for wd in flat],
        compiler_params=_params(1),
        name="inproj",
    )(x2d, *const_ops)


def _swa_body(qk_ref, v_ref, cos_ref, sin_ref, qn_ref, kn_ref, sink_ref, oc_ref, kc_ref, vc_ref, kprev, vprev, *, nblk):
    j = pl.program_id(1)
    tq = ATTN_BLOCK

    @pl.when(j == 0)
    def _():
        kprev[...] = jnp.zeros_like(kprev)
        vprev[...] = jnp.zeros_like(vprev)

    blk = qk_ref[...]
    cos = cos_ref[...]
    sin = sin_ref[...]
    lane = lax.broadcasted_iota(jnp.int32, (1, LANES), 1)
    left = (lane // 32) % 2 == 0
    lo64 = lane < 64

    def norm_rope(x, g):
        x2 = x * x
        ssl = jnp.sum(jnp.where(left, x2, 0.0), axis=-1, keepdims=True)
        ssr = jnp.sum(jnp.where(left, 0.0, x2), axis=-1, keepdims=True)
        r = jnp.where(left, lax.rsqrt(ssl * (1.0 / HEAD_DIM) + RMS_EPS), lax.rsqrt(ssr * (1.0 / HEAD_DIM) + RMS_EPS))
        y = x * r * g
        return y * cos + pltpu.roll(y, 64, 1) * sin

    khat = norm_rope(blk[:, 512:640], kn_ref[...])
    v = v_ref[...]
    k_all = jnp.concatenate([kprev[...], khat], axis=0).astype(BF16)
    v_all = jnp.concatenate([vprev[...], v], axis=0).astype(BF16)
    qhat = [norm_rope(blk[:, p * LANES:(p + 1) * LANES], qn_ref[...]) * (HEAD_DIM ** -0.5) for p in range(GROUP)]
    upper = lax.broadcasted_iota(jnp.int32, (tq, tq), 1) > lax.broadcasted_iota(jnp.int32, (tq, tq), 0)
    sink = sink_ref[...][:, :, 0:1]

    for i in range(nblk):
        rows = slice(i * tq, (i + 1) * tq)
        qs = []
        for p in range(GROUP):
            qs.append(jnp.where(left, qhat[p][rows], 0.0))
            qs.append(jnp.where(left, 0.0, qhat[p][rows]))
        q_all = jnp.concatenate(qs, axis=0).astype(BF16)
        s2 = _dot_nt(q_all, k_all[i * tq:(i + 2) * tq]).reshape(N_HEADS, tq, 2 * tq)
        s_prev = s2[:, :, :tq]
        if i == 0:
            s_prev = s_prev + jnp.where(j > 0, 0.0, NEG_INF)
        s = jnp.where(upper, s_prev, s2[:, :, tq:])
        m = jnp.maximum(jnp.max(s, axis=-1, keepdims=True), sink)
        e = jnp.exp(s - m)
        inv = 1.0 / (jnp.sum(e, axis=-1, keepdims=True) + jnp.exp(sink - m))
        e2 = jnp.concatenate([jnp.where(upper, e, 0.0), jnp.where(upper, 0.0, e)], axis=-1).astype(BF16)
        o = _dot(e2.reshape(N_HEADS * tq, 2 * tq), v_all[i * tq:(i + 2) * tq]).reshape(N_HEADS, tq, LANES) * inv
        for p in range(GROUP):
            oc_ref[rows, p * LANES:(p + 1) * LANES] = jnp.where(lo64, o[2 * p], o[2 * p + 1]).astype(oc_ref.dtype)

    last = slice((nblk - 1) * tq, nblk * tq)
    kprev[...] = khat[last]
    vprev[...] = v[last]
    kc_ref[...] = khat[last]
    vc_ref[...] = v[last]


def _swa_prompt(qk, vga, cos, sin, qn, kn, sinks, batch, seq, nblk):
    ts = nblk * ATTN_BLOCK
    nb = seq // ts
    row = lambda b, j: (b * nb + j, 0)
    const_specs, const_ops = _consts((qn, kn, sinks))
    return pl.pallas_call(
        functools.partial(_swa_body, nblk=nblk),
        grid=(batch, nb),
        in_specs=[pl.BlockSpec((ts, 640), row), pl.BlockSpec((ts, LANES), row),
                  pl.BlockSpec((ts, LANES), lambda b, j: (j, 0)),
                  pl.BlockSpec((ts, LANES), lambda b, j: (j, 0))] + const_specs,
        out_specs=[pl.BlockSpec((ts, 512), row),
                   pl.BlockSpec((ATTN_BLOCK, LANES), lambda b, j: (b, 0)),
                   pl.BlockSpec((ATTN_BLOCK, LANES), lambda b, j: (b, 0))],
        out_shape=[jax.ShapeDtypeStruct((batch * seq, 512), BF16),
                   jax.ShapeDtypeStruct((batch * WINDOW, LANES), F32),
                   jax.ShapeDtypeStruct((batch * WINDOW, LANES), F32)],
        scratch_shapes=[pltpu.VMEM((ATTN_BLOCK, LANES), F32), pltpu.VMEM((ATTN_BLOCK, LANES), F32)],
        compiler_params=_params(2),
        name="swa_prompt",
    )(qk, vga, cos, sin, *const_ops)


def _split2(x):
    hi = x.astype(BF16)
    lo = (x - hi.astype(F32)).astype(BF16)
    return hi, lo


def _gla_body(gqk_ref, gv_ref, gg_ref, ga_ref, wg_ref, bg_ref, gn_ref, od_ref, st_ref, st_scr, *, tg):
    t = pl.program_id(1)
    ck = GLA_CHUNK

    @pl.when(t == 0)
    def _():
        st_scr[...] = jnp.zeros_like(st_scr)

    z = _dot(ga_ref[...].astype(BF16), wg_ref[...]) + bg_ref[...]
    la = _log_sigmoid(z) * (1.0 / GLA_GATE_NORM)
    r_i = lax.broadcasted_iota(jnp.int32, (tg, tg), 0)
    c_i = lax.broadcasted_iota(jnp.int32, (tg, tg), 1)
    same = (r_i // ck) == (c_i // ck)
    tri = jnp.where(same & (c_i <= r_i), 1.0, 0.0).astype(BF16)
    ones = jnp.where(same, 1.0, 0.0).astype(BF16)
    la_hi, la_lo = _split2(la)
    bcum = _dot(tri, la_hi) + _dot(tri, la_lo)
    btot = _dot(ones, la_hi) + _dot(ones, la_lo)

    gqk = gqk_ref[...]
    nk = GLA_HEADS * GLA_DK
    gk = gqk[:, nk:]
    qe = (gqk[:, :nk] * (GLA_DK ** -0.5)) * jnp.exp(bcum)
    ke = gk * jnp.exp(-bcum)
    kd = gk * jnp.exp(btot - bcum)
    dec = jnp.exp(btot)

    lane = lax.broadcasted_iota(jnp.int32, (1, LANES), 1)
    lo64 = lane < 64
    causal = lax.broadcasted_iota(jnp.int32, (ck, ck), 1) <= lax.broadcasted_iota(jnp.int32, (ck, ck), 0)
    u_row = lax.broadcasted_iota(jnp.int32, (2 * GLA_DV, LANES), 0)
    u_lane = lax.broadcasted_iota(jnp.int32, (2 * GLA_DV, LANES), 1)
    diag_blocks = (u_row < GLA_DV) == (u_lane < GLA_DK)
    gn = gn_ref[...]

    for c in range(tg // ck):
        rows = slice(c * ck, (c + 1) * ck)
        for p in range(GLA_HEADS // 2):
            lanes = slice(p * LANES, (p + 1) * LANES)
            vcols = slice(p * 2 * GLA_DV, (p + 1) * 2 * GLA_DV)
            qe_p = qe[rows, lanes]
            ke_p = ke[rows, lanes].astype(BF16)
            kd_p = kd[rows, lanes].astype(BF16)
            v_p = gv_ref[rows, vcols].astype(BF16)
            st = st_scr[p]
            o_halves = []
            for side in range(2):
                qm = jnp.where(lo64 == (side == 0), qe_p, 0.0).astype(BF16)
                a = jnp.where(causal, _dot_nt(qm, ke_p), 0.0).astype(BF16)
                o_halves.append(_dot(a, v_p[:, side * GLA_DV:(side + 1) * GLA_DV]))
            o = jnp.concatenate(o_halves, axis=1) + _dot_nt(qe_p.astype(BF16), st.astype(BF16))
            upd = jnp.where(diag_blocks, _dot_tn(v_p, kd_p), 0.0)
            st_scr[p] = dec[c * ck:c * ck + 1, lanes] * st + upd
            gate = gg_ref[rows, vcols]
            for side in range(2):
                hs = slice(side * GLA_DV, (side + 1) * GLA_DV)
                od_ref[rows, p * 2 * GLA_DV + side * GLA_DV:p * 2 * GLA_DV + (side + 1) * GLA_DV] = (
                    _rms(o[:, hs], gn) * _silu(gate[:, hs])).astype(od_ref.dtype)

    st_ref[...] = st_scr[...]


def _gla_prompt(gqk, gv, gg, vga, wg, bg, gn, batch, seq, tg):
    nt = seq // tg
    row = lambda b, t: (b * nt + t, 0)
    ga_row = lambda b, t: (b * nt + t, 1)
    const_specs, const_ops = _consts((wg, bg, gn))
    return pl.pallas_call(
        functools.partial(_gla_body, tg=tg),
        grid=(batch, nt),
        in_specs=[pl.BlockSpec((tg, 512), row), pl.BlockSpec((tg, 512), row), pl.BlockSpec((tg, 512), row),
                  pl.BlockSpec((tg, LANES), ga_row)] + const_specs,
        out_specs=[pl.BlockSpec((tg, 512), row),
                   pl.BlockSpec((None, 2, 2 * GLA_DV, LANES), lambda b, t: (b, 0, 0, 0))],
        out_shape=[jax.ShapeDtypeStruct((batch * seq, 512), BF16),
                   jax.ShapeDtypeStruct((batch, 2, 2 * GLA_DV, LANES), F32)],
        scratch_shapes=[pltpu.VMEM((2, 2 * GLA_DV, LANES), F32)],
        compiler_params=_params(2),
        name="gla_prompt",
    )(gqk, gv, gg, vga, *const_ops)


U_PAD = 8
G_PAD = 32
CONV_ROWS = 16


def _conv_prepare(xn, w_ref, cs_ref, ubuf, gbuf, gsh, oa_scr, tm):
    i = pl.program_id(1)
    bw = BRANCH_WIDTH

    @pl.when(i == 0)
    def _():
        ubuf[0:U_PAD, :] = jnp.zeros((U_PAD, bw), F32)
        gbuf[0:G_PAD, :] = jnp.zeros((G_PAD, bw), F32)

    @pl.when(i > 0)
    def _():
        ubuf[0:U_PAD, :] = ubuf[tm:tm + U_PAD, :]
        gbuf[0:G_PAD, :] = gbuf[tm:tm + G_PAD, :]

    sc_h = _dot(xn, w_ref[:, 0:bw])
    sc_c = _dot(xn, w_ref[:, 2 * bw:3 * bw])
    u = sc_c * sc_h
    ubuf[U_PAD:U_PAD + tm, :] = u
    conv_u = (cs_ref[0:1, :] * ubuf[U_PAD - 2:U_PAD - 2 + tm, :] + cs_ref[1:2, :] * ubuf[U_PAD - 1:U_PAD - 1 + tm, :]
              + cs_ref[2:3, :] * u)
    oa_scr[...] = (_dot(xn, w_ref[:, bw:2 * bw]) * conv_u).astype(oa_scr.dtype)

    cf_a = _dot(xn, w_ref[:, 3 * bw:4 * bw])
    cf_b = _dot(xn, w_ref[:, 4 * bw:5 * bw])
    gbuf[G_PAD:G_PAD + tm, :] = cf_a * jax.nn.sigmoid(cf_b)
    span = tm + G_PAD - SUBLANES
    for sh in range(1, SUBLANES):
        gsh[sh - 1] = gbuf[sh:sh + span, :]


def _conv_block(r0, cc_ref, gbuf, gsh, cbuf):
    base = G_PAD - (CF_CONV_W - 1)
    acc = None
    for k in range(CF_CONV_W):
        al, sh = divmod(base + k, SUBLANES)
        rows = pl.ds(pl.multiple_of(r0 + al * SUBLANES, SUBLANES), CONV_ROWS)
        w_tap = jnp.concatenate([cc_ref[k]] * (CONV_ROWS // SUBLANES), axis=0)
        tap = w_tap * (gbuf[rows, :] if sh == 0 else gsh[sh - 1, rows, :])
        acc = tap if acc is None else acc + tap
    cbuf[pl.ds(pl.multiple_of(r0, CONV_ROWS), CONV_ROWS), :] = acc


def _conv_finish(cbuf, ccb_ref, lng_ref, lnb_ref):
    c = cbuf[...] + ccb_ref[...]
    mu = jnp.mean(c, axis=-1, keepdims=True)
    xc = c - mu
    y = xc * lax.rsqrt(jnp.mean(xc * xc, axis=-1, keepdims=True) + LN_EPS) * lng_ref[...] + lnb_ref[...]
    return _silu(y)


MERGE_COLS = 256
MERGE_CHUNKS = D_MODEL // MERGE_COLS


def _merge_term(xn, o, r, c, wgt_ref, wbr_ref):
    idx = r * MERGE_CHUNKS + c
    return jax.nn.sigmoid(_dot(xn, wgt_ref[idx])) * _dot(o, wbr_ref[idx])


def _merge_out(x, xn, branches, wgt_ref, wbr_ref, wout_ref):
    y = x
    for c in range(MERGE_CHUNKS):
        merged = None
        for r, o in enumerate(branches):
            term = _merge_term(xn, o.astype(BF16), r, c, wgt_ref, wbr_ref)
            merged = term if merged is None else merged + term
        y = y + _dot(merged.astype(BF16), wout_ref[c * MERGE_COLS:(c + 1) * MERGE_COLS, :])
    return y


def _swiglu(x1, gf_ref, wfi_ref, wfo_ref):
    xn2 = _rms(x1, gf_ref[...]).astype(BF16)
    y = x1
    for c0 in range(0, D_FF, FF_CHUNK):
        h_gate = _dot(xn2, wfi_ref[:, c0:c0 + FF_CHUNK])
        h_up = _dot(xn2, wfi_ref[:, D_FF + c0:D_FF + c0 + FF_CHUNK])
        y = y + _dot((_silu(h_gate) * h_up).astype(BF16), wfo_ref[c0:c0 + FF_CHUNK, :])
    return y


def _mix_body(x_ref, oc_ref, od_ref, gm_ref, wab_ref, cs_ref, cc_ref, ccb_ref, lng_ref, lnb_ref,
              wgt_ref, wbr_ref, wout_ref, x1_ref, sc_ref, cf_ref, ubuf, gbuf, gsh, cbuf, xn_scr, oa_scr, macc, *, tm):
    xn_scr[...] = _rms(x_ref[...], gm_ref[...]).astype(BF16)
    _conv_prepare(xn_scr[...], wab_ref, cs_ref, ubuf, gbuf, gsh, oa_scr, tm)

    rows_per_step = tm // MERGE_CHUNKS

    def step(c, carry):
        xn = xn_scr[...]
        macc[c] = (_merge_term(xn, oc_ref[...], 2, c, wgt_ref, wbr_ref) + _merge_term(xn, od_ref[...], 3, c, wgt_ref, wbr_ref)
                   + _merge_term(xn, oa_scr[...], 0, c, wgt_ref, wbr_ref))
        row0 = c * rows_per_step
        for r0 in range(0, rows_per_step, CONV_ROWS):
            _conv_block(row0 + r0, cc_ref, gbuf, gsh, cbuf)
        return carry

    lax.fori_loop(0, MERGE_CHUNKS, step, 0, unroll=2)

    o_b = _conv_finish(cbuf, ccb_ref, lng_ref, lnb_ref).astype(BF16)
    xn = xn_scr[...]
    y = x_ref[...]
    for c in range(MERGE_CHUNKS):
        merged = macc[c] + _merge_term(xn, o_b, 1, c, wgt_ref, wbr_ref)
        y = y + _dot(merged.astype(BF16), wout_ref[c * MERGE_COLS:(c + 1) * MERGE_COLS, :])
    x1_ref[...] = y
    sc_ref[...] = ubuf[tm:tm + U_PAD, :]
    cf_ref[...] = gbuf[tm:tm + G_PAD, :]


def _mix_prompt(x2d, oc, od, gm, w_ab, cs, cc, ccb, lng, lnb, wgt, wbr, wout, batch, seq, tm):
    nt = seq // tm
    row = lambda b, i: (b * nt + i, 0)
    bw = BRANCH_WIDTH
    const_specs, const_ops = _consts((gm, w_ab, cs, cc, ccb, lng, lnb, wgt, wbr, wout))
    return pl.pallas_call(
        functools.partial(_mix_body, tm=tm),
        grid=(batch, nt),
        in_specs=[pl.BlockSpec((tm, D_MODEL), row), pl.BlockSpec((tm, bw), row), pl.BlockSpec((tm, bw), row)]
                 + const_specs,
        out_specs=[pl.BlockSpec((tm, D_MODEL), row),
                   pl.BlockSpec((U_PAD, bw), lambda b, i: (b, 0)), pl.BlockSpec((G_PAD, bw), lambda b, i: (b, 0))],
        out_shape=[jax.ShapeDtypeStruct((batch * seq, D_MODEL), F32),
                   jax.ShapeDtypeStruct((batch * U_PAD, bw), F32), jax.ShapeDtypeStruct((batch * G_PAD, bw), F32)],
        scratch_shapes=[pltpu.VMEM((U_PAD + tm, bw), F32), pltpu.VMEM((G_PAD + tm, bw), F32),
                        pltpu.VMEM((SUBLANES - 1, tm + G_PAD - SUBLANES, bw), F32), pltpu.VMEM((tm, bw), F32),
                        pltpu.VMEM((tm, D_MODEL), BF16), pltpu.VMEM((tm, bw), BF16),
                        pltpu.VMEM((MERGE_CHUNKS, tm, MERGE_COLS), F32)],
        compiler_params=_params(2),
        name="mix_prompt",
    )(x2d, oc, od, *const_ops)


def _ffn_body(x1_ref, gf_ref, wfi_ref, wfo_ref, y_ref):
    y_ref[...] = _swiglu(x1_ref[...], gf_ref, wfi_ref, wfo_ref)


def _ffn(x2d, gf, wfi, wfo, tm):
    n = x2d.shape[0]
    row = lambda i: (i, 0)
    const_specs, const_ops = _consts((gf, wfi, wfo))
    return pl.pallas_call(
        _ffn_body,
        grid=(n // tm,),
        in_specs=[pl.BlockSpec((tm, D_MODEL), row)] + const_specs,
        out_specs=pl.BlockSpec((tm, D_MODEL), row),
        out_shape=jax.ShapeDtypeStruct((n, D_MODEL), F32),
        compiler_params=_params(1),
        name="ffn",
    )(x2d, *const_ops)


def _merge_body(x_ref, oa_ref, ob_ref, oc_ref, od_ref, gm_ref, wgt_ref, wbr_ref, wout_ref, x1_ref):
    x = x_ref[...]
    xn = _rms(x, gm_ref[...]).astype(BF16)
    x1_ref[...] = _merge_out(x, xn, (oa_ref[...], ob_ref[...], oc_ref[...], od_ref[...]), wgt_ref, wbr_ref, wout_ref)


def _merge(x2d, oa, ob, oc, od, gm, wgt, wbr, wout, tm):
    n = x2d.shape[0]
    bw = BRANCH_WIDTH
    row = lambda i: (i, 0)
    const_specs, const_ops = _consts((gm, wgt, wbr, wout))
    return pl.pallas_call(
        _merge_body,
        grid=(n // tm,),
        in_specs=[pl.BlockSpec((tm, D_MODEL), row)] + [pl.BlockSpec((tm, bw), row)] * 4 + const_specs,
        out_specs=pl.BlockSpec((tm, D_MODEL), row),
        out_shape=jax.ShapeDtypeStruct((n, D_MODEL), F32),
        compiler_params=_params(1),
        name="merge",
    )(x2d, oa, ob, oc, od, *const_ops)


def _sample_body(ab_ref, qk_ref, v_ref, gqk_ref, gv_ref, gg_ref, ga_ref, sct_ref, cft_ref, kc_ref, vc_ref, s_ref,
                 cs_ref, cc_ref, ccb_ref, lng_ref, lnb_ref, qn_ref, kn_ref, cos_ref, sin_ref, sink_ref,
                 wg_ref, bg_ref, gn_ref,
                 oa_ref, ob_ref, oc_ref, od_ref, u_ref, g_ref, kh_ref, sn_ref,
                 qh_scr, gl_scr, *, sb):
    bw = BRANCH_WIDTH
    sc_h = ab_ref[:, 0:bw]
    sc_b = ab_ref[:, bw:2 * bw]
    sc_c = ab_ref[:, 2 * bw:3 * bw]
    u = sc_c * sc_h
    u_ref[...] = u
    oa_ref[...] = sc_b * (cs_ref[0:1, :] * sct_ref[0] + cs_ref[1:2, :] * sct_ref[1] + cs_ref[2:3, :] * u)
    g = ab_ref[:, 3 * bw:4 * bw] * jax.nn.sigmoid(ab_ref[:, 4 * bw:5 * bw])
    g_ref[...] = g
    c = ccb_ref[...] + cc_ref[CF_CONV_W - 1, 0:1, :] * g
    for k in range(CF_CONV_W - 1):
        c = c + cc_ref[k, 0:1, :] * cft_ref[k]
    mu = jnp.mean(c, axis=-1, keepdims=True)
    xc = c - mu
    ob_ref[...] = _silu(xc * lax.rsqrt(jnp.mean(xc * xc, axis=-1, keepdims=True) + LN_EPS) * lng_ref[...] + lnb_ref[...])

    lane = lax.broadcasted_iota(jnp.int32, (1, LANES), 1)
    lo64 = lane < 64
    first32 = (lane % 64) < 32
    cos = cos_ref[...]
    sin = sin_ref[...]

    def norm_rope(x, gw):
        x2 = x * x
        ssl = jnp.sum(jnp.where(lo64, x2, 0.0), axis=-1, keepdims=True)
        ssr = jnp.sum(jnp.where(lo64, 0.0, x2), axis=-1, keepdims=True)
        r = jnp.where(lo64, lax.rsqrt(ssl * (1.0 / HEAD_DIM) + RMS_EPS), lax.rsqrt(ssr * (1.0 / HEAD_DIM) + RMS_EPS))
        y = x * r * gw
        partner = jnp.where(first32, pltpu.roll(y, LANES - 32, 1), pltpu.roll(y, 32, 1))
        return y * cos + partner * sin

    for p in range(GROUP):
        qh_scr[:, p * LANES:(p + 1) * LANES] = norm_rope(qk_ref[:, p * LANES:(p + 1) * LANES], qn_ref[...])
    khat = norm_rope(qk_ref[:, 512:640], kn_ref[...])
    kh_ref[...] = khat
    qh_scr[:, 512:640] = khat

    z = _dot(ga_ref[...].astype(BF16), wg_ref[...]) + bg_ref[...]
    nk = GLA_HEADS * GLA_DK
    gl_scr[:, 0:nk] = jnp.exp(_log_sigmoid(z) * (1.0 / GLA_GATE_NORM))
    gl_scr[:, nk:2 * nk] = gqk_ref[:, 0:nk] * (GLA_DK ** -0.5)
    gl_scr[:, 2 * nk:3 * nk] = gqk_ref[:, nk:2 * nk]

    top = lax.broadcasted_iota(jnp.int32, (LANES, LANES), 0) < GLA_DK
    tile_row = lax.broadcasted_iota(jnp.int32, (SUBLANES, LANES), 0)
    ext_row = lax.broadcasted_iota(jnp.int32, (WINDOW + SUBLANES, 1), 0)
    key_bias = jnp.where((ext_row == 0) | (ext_row > WINDOW), NEG_INF, 0.0)
    sink = sink_ref[...]
    gla_cols = [[gl_scr[:, j * nk + p * LANES:j * nk + (p + 1) * LANES].T for j in range(3)]
                for p in range(GLA_HEADS // 2)]

    for b in range(sb):
        k_new = qh_scr[b:b + 1, 512:640]
        v_new = v_ref[b:b + 1, :]
        kext = jnp.concatenate([kc_ref[b], jnp.where(tile_row == 0, k_new, 0.0)], axis=0).astype(BF16)
        vext = jnp.concatenate([vc_ref[b], jnp.where(tile_row == 0, v_new, 0.0)], axis=0).astype(BF16)
        q_rows = jnp.zeros((SUBLANES, LANES), F32)
        for p in range(GROUP):
            q_row = qh_scr[b:b + 1, p * LANES:(p + 1) * LANES]
            q_rows = jnp.where(tile_row == 2 * p, jnp.where(lo64, q_row, 0.0), q_rows)
            q_rows = jnp.where(tile_row == 2 * p + 1, jnp.where(lo64, 0.0, q_row), q_rows)
        s = _dot_nt(kext, q_rows.astype(BF16)) * (HEAD_DIM ** -0.5) + key_bias
        m = jnp.maximum(jnp.max(s, axis=0, keepdims=True), sink)
        e = jnp.exp(s - m)
        prob = (e / (jnp.sum(e, axis=0, keepdims=True) + jnp.exp(sink - m))).astype(BF16)
        o = _dot_tn(prob, vext)
        for p in range(GROUP):
            oc_ref[b:b + 1, p * LANES:(p + 1) * LANES] = jnp.where(lo64, o[2 * p:2 * p + 1], o[2 * p + 1:2 * p + 2])

        for p in range(GLA_HEADS // 2):
            lanes = slice(p * LANES, (p + 1) * LANES)
            a_col, q_col, k_col = (col[:, b:b + 1] for col in gla_cols[p])
            v0 = gv_ref[b:b + 1, 2 * p * GLA_DV:(2 * p + 1) * GLA_DV]
            v1 = gv_ref[b:b + 1, (2 * p + 1) * GLA_DV:(2 * p + 2) * GLA_DV]
            vsel = jnp.where(top, jnp.broadcast_to(v0, (LANES, GLA_DV)), jnp.broadcast_to(v1, (LANES, GLA_DV)))
            s_new = a_col * s_ref[b, lanes, :] + k_col * vsel
            sn_ref[b, lanes, :] = s_new
            w = q_col * s_new
            od_ref[b:b + 1, 2 * p * GLA_DV:(2 * p + 1) * GLA_DV] = jnp.sum(w[0:GLA_DK], axis=0, keepdims=True)
            od_ref[b:b + 1, (2 * p + 1) * GLA_DV:(2 * p + 2) * GLA_DV] = jnp.sum(w[GLA_DK:], axis=0, keepdims=True)

    gn = gn_ref[...]
    for h in range(GLA_HEADS):
        hs = slice(h * GLA_DV, (h + 1) * GLA_DV)
        od_ref[:, hs] = _rms(od_ref[:, hs], gn) * _silu(gg_ref[:, hs])


def _sample_mixers(ab, qk, vga, gqk, gv, gg, sct, cft, kc, vc, s0, small, sb, layer):
    n = ab.shape[0]
    bw = BRANCH_WIDTH
    row = lambda i: (i, 0)
    ga_row = lambda i: (i, 1)
    lead3 = lambda i: (i, 0, 0)
    state_lead = lambda i: (layer, i, 0, 0)
    state_mid = lambda i: (layer, 0, i, 0)
    const_specs, const_ops = _consts(small)
    return pl.pallas_call(
        functools.partial(_sample_body, sb=sb),
        grid=(n // sb,),
        in_specs=[pl.BlockSpec((sb, W_AB), row), pl.BlockSpec((sb, 640), row), pl.BlockSpec((sb, LANES), row),
                  pl.BlockSpec((sb, 512), row),
                  pl.BlockSpec((sb, 512), row), pl.BlockSpec((sb, 512), row), pl.BlockSpec((sb, LANES), ga_row),
                  pl.BlockSpec((None, SC_CONV_W - 1, sb, bw), state_mid),
                  pl.BlockSpec((None, CF_CONV_W - 1, sb, bw), state_mid),
                  pl.BlockSpec((None, sb, WINDOW, LANES), state_lead), pl.BlockSpec((None, sb, WINDOW, LANES), state_lead),
                  pl.BlockSpec((None, sb, GLA_HEADS * GLA_DK, GLA_DV), state_lead)] + const_specs,
        out_specs=[pl.BlockSpec((sb, bw), row)] * 6 + [pl.BlockSpec((sb, LANES), row),
                                                       pl.BlockSpec((sb, GLA_HEADS * GLA_DK, GLA_DV), lead3)],
        out_shape=[jax.ShapeDtypeStruct((n, bw), F32)] * 6 + [jax.ShapeDtypeStruct((n, LANES), F32),
                                                              jax.ShapeDtypeStruct(s0.shape[1:], F32)],
        scratch_shapes=[pltpu.VMEM((sb, 640), F32), pltpu.VMEM((sb, 3 * GLA_HEADS * GLA_DK), F32)],
        compiler_params=_params(1),
        name="sample_mixers",
    )(ab, qk, vga, gqk, gv, gg, vga, sct, cft, kc, vc, s0, *const_ops)


def _lane_dims(rope_split):
    lane = np.arange(LANES)
    if rope_split:
        return (lane // 64) * 32 + lane % 32
    return lane % HEAD_DIM


def _qk_weight(w_in, rope_split):
    depth, half = w_in.shape[0], HEAD_DIM // 2
    wq = w_in[:, :, OFF_Q:OFF_K].reshape(depth, D_MODEL, N_KV_HEADS, GROUP, 2, half)
    wk = w_in[:, :, OFF_K:OFF_V].reshape(depth, D_MODEL, N_KV_HEADS, 2, half)
    if rope_split:
        wq = wq.transpose(0, 1, 3, 4, 2, 5)
        wk = wk.transpose(0, 1, 3, 2, 4)
    else:
        wq = wq.transpose(0, 1, 3, 2, 4, 5)
    return jnp.concatenate([wq.reshape(depth, D_MODEL, N_HEADS * HEAD_DIM),
                            wk.reshape(depth, D_MODEL, N_KV_HEADS * HEAD_DIM)], axis=2)


def _rope_tables(pos, dims):
    half = HEAD_DIM // 2
    inv = ROPE_THETA ** (-jnp.arange(half, dtype=F32) / half)
    ang = pos.astype(F32)[:, None] * inv[None, :]
    cos = jnp.cos(ang)[:, dims % half]
    sin = jnp.sin(ang)[:, dims % half] * jnp.where(dims < half, -1.0, 1.0).astype(F32)[None, :]
    return cos, sin


def _rest_weight(w_in):
    pad = jnp.zeros((w_in.shape[0], D_MODEL, LANES - GLA_GATE_RANK), w_in.dtype)
    return jnp.concatenate([w_in[:, :, OFF_V:OFF_GATES], pad], axis=2)


def kernel(x_prompt, x_sample, cache_swa_k, cache_swa_v, state_sconv, state_cconv, state_gla, norm_mix, w_in,
           conv_short, conv_conf, conv_conf_b, conf_ln_g, conf_ln_b, q_norm, k_norm, attn_sinks, w_gla_gate,
           b_gla_gate, gla_norm, w_branch, w_out, norm_ffn, w_ffn_in, w_ffn_out):
    batch, seq, _ = x_prompt.shape
    n_dec = x_sample.shape[0]
    depth = w_in.shape[0]
    bw = BRANCH_WIDTH

    dims_p, dims_s = _lane_dims(True), _lane_dims(False)
    cos_p, sin_p = _rope_tables(jnp.arange(seq), dims_p)
    cos_s, sin_s = _rope_tables(PAST_LEN + jnp.arange(1), dims_s)
    head_order = np.array([p + GROUP * side for p in range(GROUP) for side in range(2)])
    rest_cols = (((0, LANES), (LANES + 3 * 512, LANES)), ((LANES, 512),), ((LANES + 512, 512),), ((LANES + 1024, 512),))
    qk_cols = (((0, 640),),)
    ab_cols = (((0, W_AB),),)

    yp = x_prompt.reshape(batch * seq, D_MODEL)
    ys = x_sample.reshape(n_dec, D_MODEL)
    outs = {k: [] for k in ("sc_p", "cf_p", "k_p", "v_p", "S_p", "u_s", "g_s", "k_s", "v_s", "S_s")}

    rows3 = lambda a: a.reshape(depth, 1, -1)
    w_in_b = w_in.astype(BF16)
    w_branch_b = w_branch.astype(BF16)
    w_ab = w_in_b[:, :, :W_AB]
    w_qk_p = _qk_weight(w_in_b, True)
    w_qk_s = _qk_weight(w_in_b, False)
    w_rest = _rest_weight(w_in_b)
    w_gates = (w_in_b[:, :, OFF_GATES:].reshape(depth, D_MODEL, N_BRANCH * MERGE_CHUNKS, MERGE_COLS)
               .transpose(0, 2, 1, 3))
    wbr_c = (w_branch_b[:, 2].reshape(depth, N_KV_HEADS, GROUP, HEAD_DIM, D_MODEL).transpose(0, 2, 1, 3, 4)
             .reshape(depth, 1, bw, D_MODEL))
    wbr = (jnp.concatenate([w_branch_b[:, :2], wbr_c, w_branch_b[:, 3:]], axis=1)
           .reshape(depth, N_BRANCH, bw, MERGE_CHUNKS, MERGE_COLS).transpose(0, 1, 3, 2, 4)
           .reshape(depth, N_BRANCH * MERGE_CHUNKS, bw, MERGE_COLS))
    wout = w_out.astype(BF16)
    wfi = w_ffn_in.astype(BF16)
    wfo = w_ffn_out.astype(BF16)
    wg = jnp.concatenate([w_gla_gate, jnp.zeros((depth, LANES - GLA_GATE_RANK, GLA_HEADS * GLA_DK), F32)], axis=1).astype(BF16)
    stacked = dict(
        w_ab=w_ab, w_qk_p=w_qk_p, w_qk_s=w_qk_s, w_rest=w_rest, w_gates=w_gates, wbr=wbr, wout=wout, wfi=wfi, wfo=wfo, wg=wg,
        bg=rows3(b_gla_gate), gn=rows3(gla_norm), gm=rows3(norm_mix), gf=rows3(norm_ffn),
        qn_p=rows3(q_norm[:, dims_p]), kn_p=rows3(k_norm[:, dims_p]), qn_s=rows3(q_norm[:, dims_s]), kn_s=rows3(k_norm[:, dims_s]),
        sinks_s=attn_sinks[:, head_order].reshape(depth, 1, N_HEADS),
        sinks_p=jnp.broadcast_to(attn_sinks[:, head_order][:, :, None, None], (depth, N_HEADS, 1, LANES)),
        cs=conv_short, ccb=rows3(conv_conf_b), lng=rows3(conf_ln_g), lnb=rows3(conf_ln_b),
        cc=jnp.broadcast_to(conv_conf[:, :, None, :], (depth, CF_CONV_W, SUBLANES, bw)),
    )
    sct = state_sconv.transpose(0, 2, 1, 3)
    cft = state_cconv.transpose(0, 2, 1, 3)
    kc_s = cache_swa_k.reshape(depth, n_dec, WINDOW, LANES)
    vc_s = cache_swa_v.reshape(depth, n_dec, WINDOW, LANES)
    s0 = state_gla.reshape(depth, n_dec, GLA_HEADS * GLA_DK, GLA_DV)

    for l in range(depth):
        p = {name: _Layer(arr, l) for name, arr in stacked.items()}

        qk, vga, gqk, gv, gg = _inproj(yp, p["gm"], (p["w_qk_p"], p["w_rest"]), (qk_cols, rest_cols), TM_INPROJ)
        oc, kc, vc = _swa_prompt(qk, vga, cos_p, sin_p, p["qn_p"], p["kn_p"], p["sinks_p"], batch, seq, SWA_BLOCKS)
        od, st = _gla_prompt(gqk, gv, gg, vga, p["wg"], p["bg"], p["gn"], batch, seq, TG_GLA)
        x1, sc_new, cf_new = _mix_prompt(yp, oc, od, p["gm"], p["w_ab"], p["cs"], p["cc"], p["ccb"], p["lng"], p["lnb"],
                                         p["w_gates"], p["wbr"], p["wout"], batch, seq, TM_MIX)
        yp = _ffn(x1, p["gf"], p["wfi"], p["wfo"], TM_FFN)
        outs["sc_p"].append(sc_new)
        outs["cf_p"].append(cf_new)
        outs["k_p"].append(kc)
        outs["v_p"].append(vc)
        outs["S_p"].append(st)

        ab_s, qk_s, vga_s, gqk_s, gv_s, gg_s = _inproj(
            ys, p["gm"], (p["w_ab"], p["w_qk_s"], p["w_rest"]), (ab_cols, qk_cols, rest_cols), n_dec)
        small = (p["cs"], p["cc"], p["ccb"], p["lng"], p["lnb"], p["qn_s"], p["kn_s"], cos_s, sin_s, p["sinks_s"],
                 p["wg"], p["bg"], p["gn"])
        oa_s, ob_s, oc_s, od_s, u_s, g_s, kh_s, s_new = _sample_mixers(
            ab_s, qk_s, vga_s, gqk_s, gv_s, gg_s, sct, cft, kc_s, vc_s, s0, small, SB_SAMPLE, l)
        ys = _merge(ys, oa_s, ob_s, oc_s, od_s, p["gm"], p["w_gates"], p["wbr"], p["wout"], n_dec)
        ys = _ffn(ys, p["gf"], p["wfi"], p["wfo"], n_dec)
        outs["u_s"].append(u_s)
        outs["g_s"].append(g_s)
        outs["k_s"].append(kh_s)
        outs["v_s"].append(vga_s[:, :LANES])
        outs["S_s"].append(s_new)

    st = {k: jnp.stack(v) for k, v in outs.items()}
    sc_p = st["sc_p"].reshape(depth, batch, U_PAD, bw)[:, :, U_PAD - (SC_CONV_W - 1):]
    cf_p = st["cf_p"].reshape(depth, batch, G_PAD, bw)[:, :, G_PAD - (CF_CONV_W - 1):]
    k_p = (st["k_p"].reshape(depth, batch, WINDOW, 2, N_KV_HEADS, HEAD_DIM // 2).transpose(0, 1, 2, 4, 3, 5)
           .reshape(depth, batch, WINDOW, N_KV_HEADS, HEAD_DIM))
    v_p = st["v_p"].reshape(depth, batch, WINDOW, N_KV_HEADS, HEAD_DIM)
    sp = st["S_p"].reshape(depth, batch, GLA_HEADS // 2, 2, GLA_DV, 2, GLA_DK)
    S_p = (jnp.stack([sp[:, :, :, 0, :, 0], sp[:, :, :, 1, :, 1]], axis=3)
           .reshape(depth, batch, GLA_HEADS, GLA_DV, GLA_DK).transpose(0, 1, 2, 4, 3))
    sc_s = jnp.concatenate([state_sconv[:, :, 1:], st["u_s"][:, :, None]], axis=2)
    cf_s = jnp.concatenate([state_cconv[:, :, 1:], st["g_s"][:, :, None]], axis=2)
    k_s = jnp.concatenate([cache_swa_k[:, :, 1:], st["k_s"].reshape(depth, n_dec, 1, N_KV_HEADS, HEAD_DIM)], axis=2)
    v_s = jnp.concatenate([cache_swa_v[:, :, 1:], st["v_s"].reshape(depth, n_dec, 1, N_KV_HEADS, HEAD_DIM)], axis=2)
    S_s = st["S_s"].reshape(depth, n_dec, GLA_HEADS, GLA_DK, GLA_DV)
    return (yp.reshape(batch, seq, D_MODEL), ys.reshape(n_dec, 1, D_MODEL),
            sc_p, sc_s, cf_p, cf_s, k_p, k_s, v_p, v_s, S_p, S_s)
```

```python
import functools

import numpy as np
import jax
import jax.numpy as jnp
from jax import lax
from jax.experimental import pallas as pl
from jax.experimental.pallas import tpu as pltpu

F32 = jnp.float32
BF16 = jnp.bfloat16

D_MODEL = 1024
PAST_LEN = 16384
N_BRANCH = 4
BRANCH_WIDTH = D_MODEL // 2
SC_CONV_W = 3
CF_CONV_W = 31
N_HEADS = 8
N_KV_HEADS = 2
HEAD_DIM = 64
GROUP = N_HEADS // N_KV_HEADS
WINDOW = 128
ATTN_BLOCK = 128
ROPE_THETA = 10000.0
GLA_HEADS = 4
GLA_DK = 64
GLA_DV = BRANCH_WIDTH // GLA_HEADS
GLA_GATE_RANK = 16
GLA_GATE_NORM = 16.0
GLA_CHUNK = 64
D_FF = -(-8 * D_MODEL // (3 * 256)) * 256
NEG_INF = -1e30
RMS_EPS = 1e-6
LN_EPS = 1e-5

LANES = 128
SUBLANES = 8
FF_CHUNK = 256
VMEM_LIMIT = 56 * 1024 * 1024

TM_INPROJ = 1024
SWA_BLOCKS = 8
TG_GLA = 512
TM_MIX = 512
TM_FFN = 1024
SB_SAMPLE = 8

OFF_AB = 0
W_AB = 5 * BRANCH_WIDTH
OFF_Q = W_AB
OFF_K = OFF_Q + N_HEADS * HEAD_DIM
OFF_V = OFF_K + N_KV_HEADS * HEAD_DIM
OFF_GQ = OFF_V + N_KV_HEADS * HEAD_DIM
OFF_GK = OFF_GQ + GLA_HEADS * GLA_DK
OFF_GV = OFF_GK + GLA_HEADS * GLA_DK
OFF_GG = OFF_GV + GLA_HEADS * GLA_DV
OFF_GA = OFF_GG + GLA_HEADS * GLA_DV
OFF_GATES = OFF_GA + GLA_GATE_RANK


class _Layer:
    def __init__(self, stacked, layer):
        self.stacked, self.layer = stacked, layer


def _consts(params):
    specs, operands = [], []
    for p in params:
        if isinstance(p, _Layer):
            shape, layer = p.stacked.shape[1:], p.layer
            specs.append(pl.BlockSpec((None,) + shape, lambda *_, layer=layer, nd=len(shape): (layer,) + (0,) * nd,
                                      pipeline_mode=pl.Buffered(1)))
            operands.append(p.stacked)
        else:
            specs.append(pl.BlockSpec(p.shape, lambda *_, nd=p.ndim: (0,) * nd, pipeline_mode=pl.Buffered(1)))
            operands.append(p)
    return specs, operands


def _params(n_axes, flags=None, fuse_inputs=None):
    return pltpu.CompilerParams(dimension_semantics=("arbitrary",) * n_axes, vmem_limit_bytes=VMEM_LIMIT, flags=flags,
                                allow_input_fusion=fuse_inputs)


def _rms(x, g):
    return x * lax.rsqrt(jnp.mean(x * x, axis=-1, keepdims=True) + RMS_EPS) * g


def _dot(a, b):
    return jnp.dot(a, b, preferred_element_type=F32)


def _dot_nt(a, b):
    return lax.dot_general(a, b, (((1,), (1,)), ((), ())), preferred_element_type=F32)


def _dot_tn(a, b):
    return lax.dot_general(a, b, (((0,), (0,)), ((), ())), preferred_element_type=F32)


def _log_sigmoid(z):
    return -(jnp.maximum(-z, 0.0) + jnp.log1p(jnp.exp(-jnp.abs(z))))


def _silu(x):
    return x * jax.nn.sigmoid(x)


def _inproj_body(x_ref, g_ref, *refs, cols):
    w_refs, out_refs = refs[:len(cols)], list(refs[len(cols):])
    xn = _rms(x_ref[...], g_ref[...]).astype(BF16)
    for w_ref, outputs in zip(w_refs, cols):
        for ranges in outputs:
            parts = [w_ref[:, off:off + wd] for off, wd in ranges]
            w = parts[0] if len(parts) == 1 else jnp.concatenate(parts, axis=1)
            out_refs.pop(0)[...] = _dot(xn, w)


def _inproj(x2d, g, weights, cols, tm):
    n = x2d.shape[0]
    flat = [sum(wd for _, wd in ranges) for outputs in cols for ranges in outputs]
    const_specs, const_ops = _consts((g,) + tuple(weights))
    return pl.pallas_call(
        functools.partial(_inproj_body, cols=cols),
        grid=(n // tm,),
        in_specs=[pl.BlockSpec((tm, D_MODEL), lambda i: (i, 0))] + const_specs,
        out_specs=[pl.BlockSpec((tm, wd), lambda i: (i, 0)) for wd in flat],
        out_shape=[jax.ShapeDtypeStruct((n, wd), F32) for wd in flat],
        compiler_params=_params(1, fuse_inputs=[False, False] + [True] * len(weights)),
        name="inproj",
    )(x2d, *const_ops)


def _swa_body(qk_ref, v_ref, cos_ref, sin_ref, qn_ref, kn_ref, sink_ref, oc_ref, kc_ref, vc_ref, kprev, vprev, *, nblk):
    j = pl.program_id(1)
    tq = ATTN_BLOCK

    @pl.when(j == 0)
    def _():
        kprev[...] = jnp.zeros_like(kprev)
        vprev[...] = jnp.zeros_like(vprev)

    blk = qk_ref[...]
    cos = cos_ref[...]
    sin = sin_ref[...]
    lane = lax.broadcasted_iota(jnp.int32, (1, LANES), 1)
    left = (lane // 32) % 2 == 0
    lo64 = lane < 64

    def norm_rope(x, g):
        x2 = x * x
        ssl = jnp.sum(jnp.where(left, x2, 0.0), axis=-1, keepdims=True)
        ssr = jnp.sum(jnp.where(left, 0.0, x2), axis=-1, keepdims=True)
        r = jnp.where(left, lax.rsqrt(ssl * (1.0 / HEAD_DIM) + RMS_EPS), lax.rsqrt(ssr * (1.0 / HEAD_DIM) + RMS_EPS))
        y = x * r * g
        return y * cos + pltpu.roll(y, 64, 1) * sin

    khat = norm_rope(blk[:, 512:640], kn_ref[...])
    v = v_ref[...]
    k_all = jnp.concatenate([kprev[...], khat], axis=0).astype(BF16)
    v_all = jnp.concatenate([vprev[...], v], axis=0).astype(BF16)
    qhat = [norm_rope(blk[:, p * LANES:(p + 1) * LANES], qn_ref[...]) * (HEAD_DIM ** -0.5) for p in range(GROUP)]
    upper = lax.broadcasted_iota(jnp.int32, (tq, tq), 1) > lax.broadcasted_iota(jnp.int32, (tq, tq), 0)
    sink = sink_ref[...][:, :, 0:1]

    for i in range(nblk):
        rows = slice(i * tq, (i + 1) * tq)
        qs = []
        for p in range(GROUP):
            qs.append(jnp.where(left, qhat[p][rows], 0.0))
            qs.append(jnp.where(left, 0.0, qhat[p][rows]))
        q_all = jnp.concatenate(qs, axis=0).astype(BF16)
        s2 = _dot_nt(q_all, k_all[i * tq:(i + 2) * tq]).reshape(N_HEADS, tq, 2 * tq)
        s_prev = s2[:, :, :tq]
        if i == 0:
            s_prev = s_prev + jnp.where(j > 0, 0.0, NEG_INF)
        s = jnp.where(upper, s_prev, s2[:, :, tq:])
        m = jnp.maximum(jnp.max(s, axis=-1, keepdims=True), sink)
        e = jnp.exp(s - m)
        inv = 1.0 / (jnp.sum(e, axis=-1, keepdims=True) + jnp.exp(sink - m))
        e2 = jnp.concatenate([jnp.where(upper, e, 0.0), jnp.where(upper, 0.0, e)], axis=-1).astype(BF16)
        o = _dot(e2.reshape(N_HEADS * tq, 2 * tq), v_all[i * tq:(i + 2) * tq]).reshape(N_HEADS, tq, LANES) * inv
        for p in range(GROUP):
            oc_ref[rows, p * LANES:(p + 1) * LANES] = jnp.where(lo64, o[2 * p], o[2 * p + 1]).astype(oc_ref.dtype)

    last = slice((nblk - 1) * tq, nblk * tq)
    kprev[...] = khat[last]
    vprev[...] = v[last]
    kc_ref[...] = khat[last]
    vc_ref[...] = v[last]


def _swa_prompt(qk, vga, cos, sin, qn, kn, sinks, batch, seq, nblk):
    ts = nblk * ATTN_BLOCK
    nb = seq // ts
    row = lambda b, j: (b * nb + j, 0)
    const_specs, const_ops = _consts((qn, kn, sinks))
    return pl.pallas_call(
        functools.partial(_swa_body, nblk=nblk),
        grid=(batch, nb),
        in_specs=[pl.BlockSpec((ts, 640), row), pl.BlockSpec((ts, LANES), row),
                  pl.BlockSpec((ts, LANES), lambda b, j: (j, 0)),
                  pl.BlockSpec((ts, LANES), lambda b, j: (j, 0))] + const_specs,
        out_specs=[pl.BlockSpec((ts, 512), row),
                   pl.BlockSpec((ATTN_BLOCK, LANES), lambda b, j: (b, 0)),
                   pl.BlockSpec((ATTN_BLOCK, LANES), lambda b, j: (b, 0))],
        out_shape=[jax.ShapeDtypeStruct((batch * seq, 512), BF16),
                   jax.ShapeDtypeStruct((batch * WINDOW, LANES), F32),
                   jax.ShapeDtypeStruct((batch * WINDOW, LANES), F32)],
        scratch_shapes=[pltpu.VMEM((ATTN_BLOCK, LANES), F32), pltpu.VMEM((ATTN_BLOCK, LANES), F32)],
        compiler_params=_params(2),
        name="swa_prompt",
    )(qk, vga, cos, sin, *const_ops)


def _split2(x):
    hi = x.astype(BF16)
    lo = (x - hi.astype(F32)).astype(BF16)
    return hi, lo


def _gla_body(gqk_ref, gv_ref, gg_ref, ga_ref, wg_ref, bg_ref, gn_ref, od_ref, st_ref, st_scr, *, tg):
    t = pl.program_id(1)
    ck = GLA_CHUNK

    @pl.when(t == 0)
    def _():
        st_scr[...] = jnp.zeros_like(st_scr)

    z = _dot(ga_ref[...].astype(BF16), wg_ref[...]) + bg_ref[...]
    la = _log_sigmoid(z) * (1.0 / GLA_GATE_NORM)
    r_i = lax.broadcasted_iota(jnp.int32, (tg, tg), 0)
    c_i = lax.broadcasted_iota(jnp.int32, (tg, tg), 1)
    same = (r_i // ck) == (c_i // ck)
    tri = jnp.where(same & (c_i <= r_i), 1.0, 0.0).astype(BF16)
    ones = jnp.where(same, 1.0, 0.0).astype(BF16)
    la_hi, la_lo = _split2(la)
    bcum = _dot(tri, la_hi) + _dot(tri, la_lo)
    btot = _dot(ones, la_hi) + _dot(ones, la_lo)

    gqk = gqk_ref[...]
    nk = GLA_HEADS * GLA_DK
    gk = gqk[:, nk:]
    qe = (gqk[:, :nk] * (GLA_DK ** -0.5)) * jnp.exp(bcum)
    ke = gk * jnp.exp(-bcum)
    kd = gk * jnp.exp(btot - bcum)
    dec = jnp.exp(btot)

    lane = lax.broadcasted_iota(jnp.int32, (1, LANES), 1)
    lo64 = lane < 64
    causal = lax.broadcasted_iota(jnp.int32, (ck, ck), 1) <= lax.broadcasted_iota(jnp.int32, (ck, ck), 0)
    u_row = lax.broadcasted_iota(jnp.int32, (2 * GLA_DV, LANES), 0)
    u_lane = lax.broadcasted_iota(jnp.int32, (2 * GLA_DV, LANES), 1)
    diag_blocks = (u_row < GLA_DV) == (u_lane < GLA_DK)
    gn = gn_ref[...]

    for c in range(tg // ck):
        rows = slice(c * ck, (c + 1) * ck)
        for p in range(GLA_HEADS // 2):
            lanes = slice(p * LANES, (p + 1) * LANES)
            vcols = slice(p * 2 * GLA_DV, (p + 1) * 2 * GLA_DV)
            qe_p = qe[rows, lanes]
            ke_p = ke[rows, lanes].astype(BF16)
            kd_p = kd[rows, lanes].astype(BF16)
            v_p = gv_ref[rows, vcols].astype(BF16)
            st = st_scr[p]
            o_halves = []
            for side in range(2):
                qm = jnp.where(lo64 == (side == 0), qe_p, 0.0).astype(BF16)
                a = jnp.where(causal, _dot_nt(qm, ke_p), 0.0).astype(BF16)
                o_halves.append(_dot(a, v_p[:, side * GLA_DV:(side + 1) * GLA_DV]))
            o = jnp.concatenate(o_halves, axis=1) + _dot_nt(qe_p.astype(BF16), st.astype(BF16))
            upd = jnp.where(diag_blocks, _dot_tn(v_p, kd_p), 0.0)
            st_scr[p] = dec[c * ck:c * ck + 1, lanes] * st + upd
            gate = gg_ref[rows, vcols]
            for side in range(2):
                hs = slice(side * GLA_DV, (side + 1) * GLA_DV)
                od_ref[rows, p * 2 * GLA_DV + side * GLA_DV:p * 2 * GLA_DV + (side + 1) * GLA_DV] = (
                    _rms(o[:, hs], gn) * _silu(gate[:, hs])).astype(od_ref.dtype)

    st_ref[...] = st_scr[...]


def _gla_prompt(gqk, gv, gg, vga, wg, bg, gn, batch, seq, tg):
    nt = seq // tg
    row = lambda b, t: (b * nt + t, 0)
    ga_row = lambda b, t: (b * nt + t, 1)
    const_specs, const_ops = _consts((wg, bg, gn))
    return pl.pallas_call(
        functools.partial(_gla_body, tg=tg),
        grid=(batch, nt),
        in_specs=[pl.BlockSpec((tg, 512), row), pl.BlockSpec((tg, 512), row), pl.BlockSpec((tg, 512), row),
                  pl.BlockSpec((tg, LANES), ga_row)] + const_specs,
        out_specs=[pl.BlockSpec((tg, 512), row),
                   pl.BlockSpec((None, 2, 2 * GLA_DV, LANES), lambda b, t: (b, 0, 0, 0))],
        out_shape=[jax.ShapeDtypeStruct((batch * seq, 512), BF16),
                   jax.ShapeDtypeStruct((batch, 2, 2 * GLA_DV, LANES), F32)],
        scratch_shapes=[pltpu.VMEM((2, 2 * GLA_DV, LANES), F32)],
        compiler_params=_params(2),
        name="gla_prompt",
    )(gqk, gv, gg, vga, *const_ops)


U_PAD = 8
G_PAD = 32
CONV_ROWS = 16


def _conv_prepare(xn, w_ref, cs_ref, ubuf, gbuf, gsh, oa_scr, tm):
    i = pl.program_id(1)
    bw = BRANCH_WIDTH

    @pl.when(i == 0)
    def _():
        ubuf[0:U_PAD, :] = jnp.zeros((U_PAD, bw), F32)
        gbuf[0:G_PAD, :] = jnp.zeros((G_PAD, bw), F32)

    @pl.when(i > 0)
    def _():
        ubuf[0:U_PAD, :] = ubuf[tm:tm + U_PAD, :]
        gbuf[0:G_PAD, :] = gbuf[tm:tm + G_PAD, :]

    sc_h = _dot(xn, w_ref[:, 0:bw])
    sc_c = _dot(xn, w_ref[:, 2 * bw:3 * bw])
    u = sc_c * sc_h
    ubuf[U_PAD:U_PAD + tm, :] = u
    conv_u = (cs_ref[0:1, :] * ubuf[U_PAD - 2:U_PAD - 2 + tm, :] + cs_ref[1:2, :] * ubuf[U_PAD - 1:U_PAD - 1 + tm, :]
              + cs_ref[2:3, :] * u)
    oa_scr[...] = (_dot(xn, w_ref[:, bw:2 * bw]) * conv_u).astype(oa_scr.dtype)

    cf_a = _dot(xn, w_ref[:, 3 * bw:4 * bw])
    cf_b = _dot(xn, w_ref[:, 4 * bw:5 * bw])
    gbuf[G_PAD:G_PAD + tm, :] = cf_a * jax.nn.sigmoid(cf_b)
    span = tm + G_PAD - SUBLANES
    for sh in range(1, SUBLANES):
        gsh[sh - 1] = gbuf[sh:sh + span, :]


def _conv_block(r0, cc_ref, gbuf, gsh, cbuf):
    base = G_PAD - (CF_CONV_W - 1)
    acc = None
    for k in range(CF_CONV_W):
        al, sh = divmod(base + k, SUBLANES)
        rows = pl.ds(pl.multiple_of(r0 + al * SUBLANES, SUBLANES), CONV_ROWS)
        w_tap = jnp.concatenate([cc_ref[k]] * (CONV_ROWS // SUBLANES), axis=0)
        tap = w_tap * (gbuf[rows, :] if sh == 0 else gsh[sh - 1, rows, :])
        acc = tap if acc is None else acc + tap
    cbuf[pl.ds(pl.multiple_of(r0, CONV_ROWS), CONV_ROWS), :] = acc


def _conv_finish(cbuf, ccb_ref, lng_ref, lnb_ref):
    c = cbuf[...] + ccb_ref[...]
    mu = jnp.mean(c, axis=-1, keepdims=True)
    xc = c - mu
    y = xc * lax.rsqrt(jnp.mean(xc * xc, axis=-1, keepdims=True) + LN_EPS) * lng_ref[...] + lnb_ref[...]
    return _silu(y)


MERGE_COLS = 256
MERGE_CHUNKS = D_MODEL // MERGE_COLS


def _merge_term(xn, o, r, c, wgt_ref, wbr_ref):
    idx = r * MERGE_CHUNKS + c
    return jax.nn.sigmoid(_dot(xn, wgt_ref[idx])) * _dot(o, wbr_ref[idx])


def _merge_out(x, xn, branches, wgt_ref, wbr_ref, wout_ref):
    y = x
    for c in range(MERGE_CHUNKS):
        merged = None
        for r, o in enumerate(branches):
            term = _merge_term(xn, o.astype(BF16), r, c, wgt_ref, wbr_ref)
            merged = term if merged is None else merged + term
        y = y + _dot(merged.astype(BF16), wout_ref[c * MERGE_COLS:(c + 1) * MERGE_COLS, :])
    return y


def _swiglu(x1, gf_ref, wfi_ref, wfo_ref):
    xn2 = _rms(x1, gf_ref[...]).astype(BF16)
    y = x1
    for c0 in range(0, D_FF, FF_CHUNK):
        h_gate = _dot(xn2, wfi_ref[:, c0:c0 + FF_CHUNK])
        h_up = _dot(xn2, wfi_ref[:, D_FF + c0:D_FF + c0 + FF_CHUNK])
        y = y + _dot((_silu(h_gate) * h_up).astype(BF16), wfo_ref[c0:c0 + FF_CHUNK, :])
    return y


def _mix_body(x_ref, oc_ref, od_ref, gm_ref, wab_ref, cs_ref, cc_ref, ccb_ref, lng_ref, lnb_ref,
              wgt_ref, wbr_ref, wout_ref, x1_ref, sc_ref, cf_ref, ubuf, gbuf, gsh, cbuf, xn_scr, oa_scr, macc, *, tm):
    xn_scr[...] = _rms(x_ref[...], gm_ref[...]).astype(BF16)
    _conv_prepare(xn_scr[...], wab_ref, cs_ref, ubuf, gbuf, gsh, oa_scr, tm)

    rows_per_step = tm // MERGE_CHUNKS

    def step(c, carry):
        xn = xn_scr[...]
        macc[c] = (_merge_term(xn, oc_ref[...], 2, c, wgt_ref, wbr_ref) + _merge_term(xn, od_ref[...], 3, c, wgt_ref, wbr_ref)
                   + _merge_term(xn, oa_scr[...], 0, c, wgt_ref, wbr_ref))
        row0 = c * rows_per_step
        for r0 in range(0, rows_per_step, CONV_ROWS):
            _conv_block(row0 + r0, cc_ref, gbuf, gsh, cbuf)
        return carry

    lax.fori_loop(0, MERGE_CHUNKS, step, 0, unroll=2)

    o_b = _conv_finish(cbuf, ccb_ref, lng_ref, lnb_ref).astype(BF16)
    xn = xn_scr[...]
    y = x_ref[...]
    for c in range(MERGE_CHUNKS):
        merged = macc[c] + _merge_term(xn, o_b, 1, c, wgt_ref, wbr_ref)
        y = y + _dot(merged.astype(BF16), wout_ref[c * MERGE_COLS:(c + 1) * MERGE_COLS, :])
    x1_ref[...] = y
    sc_ref[...] = ubuf[tm:tm + U_PAD, :]
    cf_ref[...] = gbuf[tm:tm + G_PAD, :]


def _mix_prompt(x2d, oc, od, gm, w_ab, cs, cc, ccb, lng, lnb, wgt, wbr, wout, batch, seq, tm):
    nt = seq // tm
    row = lambda b, i: (b * nt + i, 0)
    bw = BRANCH_WIDTH
    const_specs, const_ops = _consts((gm, w_ab, cs, cc, ccb, lng, lnb, wgt, wbr, wout))
    return pl.pallas_call(
        functools.partial(_mix_body, tm=tm),
        grid=(batch, nt),
        in_specs=[pl.BlockSpec((tm, D_MODEL), row), pl.BlockSpec((tm, bw), row), pl.BlockSpec((tm, bw), row)]
                 + const_specs,
        out_specs=[pl.BlockSpec((tm, D_MODEL), row),
                   pl.BlockSpec((U_PAD, bw), lambda b, i: (b, 0)), pl.BlockSpec((G_PAD, bw), lambda b, i: (b, 0))],
        out_shape=[jax.ShapeDtypeStruct((batch * seq, D_MODEL), F32),
                   jax.ShapeDtypeStruct((batch * U_PAD, bw), F32), jax.ShapeDtypeStruct((batch * G_PAD, bw), F32)],
        scratch_shapes=[pltpu.VMEM((U_PAD + tm, bw), F32), pltpu.VMEM((G_PAD + tm, bw), F32),
                        pltpu.VMEM((SUBLANES - 1, tm + G_PAD - SUBLANES, bw), F32), pltpu.VMEM((tm, bw), F32),
                        pltpu.VMEM((tm, D_MODEL), BF16), pltpu.VMEM((tm, bw), BF16),
                        pltpu.VMEM((MERGE_CHUNKS, tm, MERGE_COLS), F32)],
        compiler_params=_params(2, fuse_inputs=[False] * 4 + [True] + [False] * 5 + [True] * 3),
        name="mix_prompt",
    )(x2d, oc, od, *const_ops)


def _ffn_body(x1_ref, gf_ref, wfi_ref, wfo_ref, y_ref):
    y_ref[...] = _swiglu(x1_ref[...], gf_ref, wfi_ref, wfo_ref)


def _ffn(x2d, gf, wfi, wfo, tm):
    n = x2d.shape[0]
    row = lambda i: (i, 0)
    const_specs, const_ops = _consts((gf, wfi, wfo))
    return pl.pallas_call(
        _ffn_body,
        grid=(n // tm,),
        in_specs=[pl.BlockSpec((tm, D_MODEL), row)] + const_specs,
        out_specs=pl.BlockSpec((tm, D_MODEL), row),
        out_shape=jax.ShapeDtypeStruct((n, D_MODEL), F32),
        compiler_params=_params(1, fuse_inputs=[False, False, True, True]),
        name="ffn",
    )(x2d, *const_ops)


def _merge_body(x_ref, oa_ref, ob_ref, oc_ref, od_ref, gm_ref, wgt_ref, wbr_ref, wout_ref, x1_ref):
    x = x_ref[...]
    xn = _rms(x, gm_ref[...]).astype(BF16)
    x1_ref[...] = _merge_out(x, xn, (oa_ref[...], ob_ref[...], oc_ref[...], od_ref[...]), wgt_ref, wbr_ref, wout_ref)


def _merge(x2d, oa, ob, oc, od, gm, wgt, wbr, wout, tm):
    n = x2d.shape[0]
    bw = BRANCH_WIDTH
    row = lambda i: (i, 0)
    const_specs, const_ops = _consts((gm, wgt, wbr, wout))
    return pl.pallas_call(
        _merge_body,
        grid=(n // tm,),
        in_specs=[pl.BlockSpec((tm, D_MODEL), row)] + [pl.BlockSpec((tm, bw), row)] * 4 + const_specs,
        out_specs=pl.BlockSpec((tm, D_MODEL), row),
        out_shape=jax.ShapeDtypeStruct((n, D_MODEL), F32),
        compiler_params=_params(1, fuse_inputs=[False] * 6 + [True] * 3),
        name="merge",
    )(x2d, oa, ob, oc, od, *const_ops)


def _sample_body(ab_ref, qk_ref, v_ref, gqk_ref, gv_ref, gg_ref, ga_ref, sct_ref, cft_ref, kc_ref, vc_ref, s_ref,
                 cs_ref, cc_ref, ccb_ref, lng_ref, lnb_ref, qn_ref, kn_ref, cos_ref, sin_ref, sink_ref,
                 wg_ref, bg_ref, gn_ref,
                 oa_ref, ob_ref, oc_ref, od_ref, u_ref, g_ref, kh_ref, sn_ref,
                 qh_scr, gl_scr, *, sb):
    bw = BRANCH_WIDTH
    sc_h = ab_ref[:, 0:bw]
    sc_b = ab_ref[:, bw:2 * bw]
    sc_c = ab_ref[:, 2 * bw:3 * bw]
    u = sc_c * sc_h
    u_ref[...] = u
    oa_ref[...] = sc_b * (cs_ref[0:1, :] * sct_ref[0] + cs_ref[1:2, :] * sct_ref[1] + cs_ref[2:3, :] * u)
    g = ab_ref[:, 3 * bw:4 * bw] * jax.nn.sigmoid(ab_ref[:, 4 * bw:5 * bw])
    g_ref[...] = g
    c = ccb_ref[...] + cc_ref[CF_CONV_W - 1, 0:1, :] * g
    for k in range(CF_CONV_W - 1):
        c = c + cc_ref[k, 0:1, :] * cft_ref[k]
    mu = jnp.mean(c, axis=-1, keepdims=True)
    xc = c - mu
    ob_ref[...] = _silu(xc * lax.rsqrt(jnp.mean(xc * xc, axis=-1, keepdims=True) + LN_EPS) * lng_ref[...] + lnb_ref[...])

    lane = lax.broadcasted_iota(jnp.int32, (1, LANES), 1)
    lo64 = lane < 64
    first32 = (lane % 64) < 32
    cos = cos_ref[...]
    sin = sin_ref[...]

    def norm_rope(x, gw):
        x2 = x * x
        ssl = jnp.sum(jnp.where(lo64, x2, 0.0), axis=-1, keepdims=True)
        ssr = jnp.sum(jnp.where(lo64, 0.0, x2), axis=-1, keepdims=True)
        r = jnp.where(lo64, lax.rsqrt(ssl * (1.0 / HEAD_DIM) + RMS_EPS), lax.rsqrt(ssr * (1.0 / HEAD_DIM) + RMS_EPS))
        y = x * r * gw
        partner = jnp.where(first32, pltpu.roll(y, LANES - 32, 1), pltpu.roll(y, 32, 1))
        return y * cos + partner * sin

    for p in range(GROUP):
        qh_scr[:, p * LANES:(p + 1) * LANES] = norm_rope(qk_ref[:, p * LANES:(p + 1) * LANES], qn_ref[...])
    khat = norm_rope(qk_ref[:, 512:640], kn_ref[...])
    kh_ref[...] = khat
    qh_scr[:, 512:640] = khat

    z = _dot(ga_ref[...].astype(BF16), wg_ref[...]) + bg_ref[...]
    nk = GLA_HEADS * GLA_DK
    gl_scr[:, 0:nk] = jnp.exp(_log_sigmoid(z) * (1.0 / GLA_GATE_NORM))
    gl_scr[:, nk:2 * nk] = gqk_ref[:, 0:nk] * (GLA_DK ** -0.5)
    gl_scr[:, 2 * nk:3 * nk] = gqk_ref[:, nk:2 * nk]

    top = lax.broadcasted_iota(jnp.int32, (LANES, LANES), 0) < GLA_DK
    tile_row = lax.broadcasted_iota(jnp.int32, (SUBLANES, LANES), 0)
    ext_row = lax.broadcasted_iota(jnp.int32, (WINDOW + SUBLANES, 1), 0)
    key_bias = jnp.where((ext_row == 0) | (ext_row > WINDOW), NEG_INF, 0.0)
    sink = sink_ref[...]
    gla_cols = [[gl_scr[:, j * nk + p * LANES:j * nk + (p + 1) * LANES].T for j in range(3)]
                for p in range(GLA_HEADS // 2)]

    for b in range(sb):
        k_new = qh_scr[b:b + 1, 512:640]
        v_new = v_ref[b:b + 1, :]
        kext = jnp.concatenate([kc_ref[b], jnp.where(tile_row == 0, k_new, 0.0)], axis=0).astype(BF16)
        vext = jnp.concatenate([vc_ref[b], jnp.where(tile_row == 0, v_new, 0.0)], axis=0).astype(BF16)
        q_rows = jnp.zeros((SUBLANES, LANES), F32)
        for p in range(GROUP):
            q_row = qh_scr[b:b + 1, p * LANES:(p + 1) * LANES]
            q_rows = jnp.where(tile_row == 2 * p, jnp.where(lo64, q_row, 0.0), q_rows)
            q_rows = jnp.where(tile_row == 2 * p + 1, jnp.where(lo64, 0.0, q_row), q_rows)
        s = _dot_nt(kext, q_rows.astype(BF16)) * (HEAD_DIM ** -0.5) + key_bias
        m = jnp.maximum(jnp.max(s, axis=0, keepdims=True), sink)
        e = jnp.exp(s - m)
        prob = (e / (jnp.sum(e, axis=0, keepdims=True) + jnp.exp(sink - m))).astype(BF16)
        o = _dot_tn(prob, vext)
        for p in range(GROUP):
            oc_ref[b:b + 1, p * LANES:(p + 1) * LANES] = jnp.where(lo64, o[2 * p:2 * p + 1], o[2 * p + 1:2 * p + 2])

        for p in range(GLA_HEADS // 2):
            lanes = slice(p * LANES, (p + 1) * LANES)
            a_col, q_col, k_col = (col[:, b:b + 1] for col in gla_cols[p])
            v0 = gv_ref[b:b + 1, 2 * p * GLA_DV:(2 * p + 1) * GLA_DV]
            v1 = gv_ref[b:b + 1, (2 * p + 1) * GLA_DV:(2 * p + 2) * GLA_DV]
            vsel = jnp.where(top, jnp.broadcast_to(v0, (LANES, GLA_DV)), jnp.broadcast_to(v1, (LANES, GLA_DV)))
            s_new = a_col * s_ref[b, lanes, :] + k_col * vsel
            sn_ref[b, lanes, :] = s_new
            w = q_col * s_new
            od_ref[b:b + 1, 2 * p * GLA_DV:(2 * p + 1) * GLA_DV] = jnp.sum(w[0:GLA_DK], axis=0, keepdims=True)
            od_ref[b:b + 1, (2 * p + 1) * GLA_DV:(2 * p + 2) * GLA_DV] = jnp.sum(w[GLA_DK:], axis=0, keepdims=True)

    gn = gn_ref[...]
    for h in range(GLA_HEADS):
        hs = slice(h * GLA_DV, (h + 1) * GLA_DV)
        od_ref[:, hs] = _rms(od_ref[:, hs], gn) * _silu(gg_ref[:, hs])


def _sample_mixers(ab, qk, vga, gqk, gv, gg, sct, cft, kc, vc, s0, small, sb, layer):
    n = ab.shape[0]
    bw = BRANCH_WIDTH
    row = lambda i: (i, 0)
    ga_row = lambda i: (i, 1)
    lead3 = lambda i: (i, 0, 0)
    state_lead = lambda i: (layer, i, 0, 0)
    state_mid = lambda i: (layer, 0, i, 0)
    const_specs, const_ops = _consts(small)
    return pl.pallas_call(
        functools.partial(_sample_body, sb=sb),
        grid=(n // sb,),
        in_specs=[pl.BlockSpec((sb, W_AB), row), pl.BlockSpec((sb, 640), row), pl.BlockSpec((sb, LANES), row),
                  pl.BlockSpec((sb, 512), row),
                  pl.BlockSpec((sb, 512), row), pl.BlockSpec((sb, 512), row), pl.BlockSpec((sb, LANES), ga_row),
                  pl.BlockSpec((None, SC_CONV_W - 1, sb, bw), state_mid),
                  pl.BlockSpec((None, CF_CONV_W - 1, sb, bw), state_mid),
                  pl.BlockSpec((None, sb, WINDOW, LANES), state_lead), pl.BlockSpec((None, sb, WINDOW, LANES), state_lead),
                  pl.BlockSpec((None, sb, GLA_HEADS * GLA_DK, GLA_DV), state_lead)] + const_specs,
        out_specs=[pl.BlockSpec((sb, bw), row)] * 6 + [pl.BlockSpec((sb, LANES), row),
                                                       pl.BlockSpec((sb, GLA_HEADS * GLA_DK, GLA_DV), lead3)],
        out_shape=[jax.ShapeDtypeStruct((n, bw), F32)] * 6 + [jax.ShapeDtypeStruct((n, LANES), F32),
                                                              jax.ShapeDtypeStruct(s0.shape[1:], F32)],
        scratch_shapes=[pltpu.VMEM((sb, 640), F32), pltpu.VMEM((sb, 3 * GLA_HEADS * GLA_DK), F32)],
        compiler_params=_params(1),
        name="sample_mixers",
    )(ab, qk, vga, gqk, gv, gg, vga, sct, cft, kc, vc, s0, *const_ops)


def _lane_dims(rope_split):
    lane = np.arange(LANES)
    if rope_split:
        return (lane // 64) * 32 + lane % 32
    return lane % HEAD_DIM


def _qk_weight(w_in, rope_split):
    depth, half = w_in.shape[0], HEAD_DIM // 2
    wq = w_in[:, :, OFF_Q:OFF_K].reshape(depth, D_MODEL, N_KV_HEADS, GROUP, 2, half)
    wk = w_in[:, :, OFF_K:OFF_V].reshape(depth, D_MODEL, N_KV_HEADS, 2, half)
    if rope_split:
        wq = wq.transpose(0, 1, 3, 4, 2, 5)
        wk = wk.transpose(0, 1, 3, 2, 4)
    else:
        wq = wq.transpose(0, 1, 3, 2, 4, 5)
    return jnp.concatenate([wq.reshape(depth, D_MODEL, N_HEADS * HEAD_DIM),
                            wk.reshape(depth, D_MODEL, N_KV_HEADS * HEAD_DIM)], axis=2)


def _rope_tables(pos, dims):
    half = HEAD_DIM // 2
    inv = ROPE_THETA ** (-jnp.arange(half, dtype=F32) / half)
    ang = pos.astype(F32)[:, None] * inv[None, :]
    cos = jnp.cos(ang)[:, dims % half]
    sin = jnp.sin(ang)[:, dims % half] * jnp.where(dims < half, -1.0, 1.0).astype(F32)[None, :]
    return cos, sin


def _rest_weight(w_in):
    pad = jnp.zeros((w_in.shape[0], D_MODEL, LANES - GLA_GATE_RANK), w_in.dtype)
    return jnp.concatenate([w_in[:, :, OFF_V:OFF_GATES], pad], axis=2)


def kernel(x_prompt, x_sample, cache_swa_k, cache_swa_v, state_sconv, state_cconv, state_gla, norm_mix, w_in,
           conv_short, conv_conf, conv_conf_b, conf_ln_g, conf_ln_b, q_norm, k_norm, attn_sinks, w_gla_gate,
           b_gla_gate, gla_norm, w_branch, w_out, norm_ffn, w_ffn_in, w_ffn_out):
    batch, seq, _ = x_prompt.shape
    n_dec = x_sample.shape[0]
    depth = w_in.shape[0]
    bw = BRANCH_WIDTH

    dims_p, dims_s = _lane_dims(True), _lane_dims(False)
    cos_p, sin_p = _rope_tables(jnp.arange(seq), dims_p)
    cos_s, sin_s = _rope_tables(PAST_LEN + jnp.arange(1), dims_s)
    head_order = np.array([p + GROUP * side for p in range(GROUP) for side in range(2)])
    rest_cols = (((0, LANES), (LANES + 3 * 512, LANES)), ((LANES, 512),), ((LANES + 512, 512),), ((LANES + 1024, 512),))
    qk_cols = (((0, 640),),)
    ab_cols = (((0, W_AB),),)

    yp = x_prompt.reshape(batch * seq, D_MODEL)
    ys = x_sample.reshape(n_dec, D_MODEL)
    outs = {k: [] for k in ("sc_p", "cf_p", "k_p", "v_p", "S_p", "u_s", "g_s", "k_s", "v_s", "S_s")}

    rows3 = lambda a: a.reshape(depth, 1, -1)
    w_ab = w_in[:, :, :W_AB].astype(BF16)
    w_qk_p = _qk_weight(w_in, True).astype(BF16)
    w_qk_s = _qk_weight(w_in, False).astype(BF16)
    w_rest = _rest_weight(w_in).astype(BF16)
    w_gates = (w_in[:, :, OFF_GATES:].reshape(depth, D_MODEL, N_BRANCH * MERGE_CHUNKS, MERGE_COLS)
               .transpose(0, 2, 1, 3).astype(BF16))
    wbr_c = (w_branch[:, 2].reshape(depth, N_KV_HEADS, GROUP, HEAD_DIM, D_MODEL).transpose(0, 2, 1, 3, 4)
             .reshape(depth, 1, bw, D_MODEL))
    wbr = (jnp.concatenate([w_branch[:, :2], wbr_c, w_branch[:, 3:]], axis=1)
           .reshape(depth, N_BRANCH, bw, MERGE_CHUNKS, MERGE_COLS).transpose(0, 1, 3, 2, 4)
           .reshape(depth, N_BRANCH * MERGE_CHUNKS, bw, MERGE_COLS).astype(BF16))
    wout = w_out.astype(BF16)
    wfi = w_ffn_in.astype(BF16)
    wfo = w_ffn_out.astype(BF16)
    wg = jnp.concatenate([w_gla_gate, jnp.zeros((depth, LANES - GLA_GATE_RANK, GLA_HEADS * GLA_DK), F32)], axis=1).astype(BF16)
    stacked = dict(
        w_ab=w_ab, w_qk_p=w_qk_p, w_qk_s=w_qk_s, w_rest=w_rest, w_gates=w_gates, wbr=wbr, wout=wout, wfi=wfi, wfo=wfo, wg=wg,
        bg=rows3(b_gla_gate), gn=rows3(gla_norm), gm=rows3(norm_mix), gf=rows3(norm_ffn),
        qn_p=rows3(q_norm[:, dims_p]), kn_p=rows3(k_norm[:, dims_p]), qn_s=rows3(q_norm[:, dims_s]), kn_s=rows3(k_norm[:, dims_s]),
        sinks_s=attn_sinks[:, head_order].reshape(depth, 1, N_HEADS),
        sinks_p=jnp.broadcast_to(attn_sinks[:, head_order][:, :, None, None], (depth, N_HEADS, 1, LANES)),
        cs=conv_short, ccb=rows3(conv_conf_b), lng=rows3(conf_ln_g), lnb=rows3(conf_ln_b),
        cc=jnp.broadcast_to(conv_conf[:, :, None, :], (depth, CF_CONV_W, SUBLANES, bw)),
    )
    sct = state_sconv.transpose(0, 2, 1, 3)
    cft = state_cconv.transpose(0, 2, 1, 3)
    kc_s = cache_swa_k.reshape(depth, n_dec, WINDOW, LANES)
    vc_s = cache_swa_v.reshape(depth, n_dec, WINDOW, LANES)
    s0 = state_gla.reshape(depth, n_dec, GLA_HEADS * GLA_DK, GLA_DV)

    for l in range(depth):
        p = {name: _Layer(arr, l) for name, arr in stacked.items()}

        qk, vga, gqk, gv, gg = _inproj(yp, p["gm"], (p["w_qk_p"], p["w_rest"]), (qk_cols, rest_cols), TM_INPROJ)
        oc, kc, vc = _swa_prompt(qk, vga, cos_p, sin_p, p["qn_p"], p["kn_p"], p["sinks_p"], batch, seq, SWA_BLOCKS)
        od, st = _gla_prompt(gqk, gv, gg, vga, p["wg"], p["bg"], p["gn"], batch, seq, TG_GLA)
        x1, sc_new, cf_new = _mix_prompt(yp, oc, od, p["gm"], p["w_ab"], p["cs"], p["cc"], p["ccb"], p["lng"], p["lnb"],
                                         p["w_gates"], p["wbr"], p["wout"], batch, seq, TM_MIX)
        yp = _ffn(x1, p["gf"], p["wfi"], p["wfo"], TM_FFN)
        outs["sc_p"].append(sc_new)
        outs["cf_p"].append(cf_new)
        outs["k_p"].append(kc)
        outs["v_p"].append(vc)
        outs["S_p"].append(st)

        ab_s, qk_s, vga_s, gqk_s, gv_s, gg_s = _inproj(
            ys, p["gm"], (p["w_ab"], p["w_qk_s"], p["w_rest"]), (ab_cols, qk_cols, rest_cols), n_dec)
        small = (p["cs"], p["cc"], p["ccb"], p["lng"], p["lnb"], p["qn_s"], p["kn_s"], cos_s, sin_s, p["sinks_s"],
                 p["wg"], p["bg"], p["gn"])
        oa_s, ob_s, oc_s, od_s, u_s, g_s, kh_s, s_new = _sample_mixers(
            ab_s, qk_s, vga_s, gqk_s, gv_s, gg_s, sct, cft, kc_s, vc_s, s0, small, SB_SAMPLE, l)
        ys = _merge(ys, oa_s, ob_s, oc_s, od_s, p["gm"], p["w_gates"], p["wbr"], p["wout"], n_dec)
        ys = _ffn(ys, p["gf"], p["wfi"], p["wfo"], n_dec)
        outs["u_s"].append(u_s)
        outs["g_s"].append(g_s)
        outs["k_s"].append(kh_s)
        outs["v_s"].append(vga_s[:, :LANES])
        outs["S_s"].append(s_new)

    st = {k: jnp.stack(v) for k, v in outs.items()}
    sc_p = st["sc_p"].reshape(depth, batch, U_PAD, bw)[:, :, U_PAD - (SC_CONV_W - 1):]
    cf_p = st["cf_p"].reshape(depth, batch, G_PAD, bw)[:, :, G_PAD - (CF_CONV_W - 1):]
    k_p = (st["k_p"].reshape(depth, batch, WINDOW, 2, N_KV_HEADS, HEAD_DIM // 2).transpose(0, 1, 2, 4, 3, 5)
           .reshape(depth, batch, WINDOW, N_KV_HEADS, HEAD_DIM))
    v_p = st["v_p"].reshape(depth, batch, WINDOW, N_KV_HEADS, HEAD_DIM)
    sp = st["S_p"].reshape(depth, batch, GLA_HEADS // 2, 2, GLA_DV, 2, GLA_DK)
    S_p = (jnp.stack([sp[:, :, :, 0, :, 0], sp[:, :, :, 1, :, 1]], axis=3)
           .reshape(depth, batch, GLA_HEADS, GLA_DV, GLA_DK).transpose(0, 1, 2, 4, 3))
    sc_s = jnp.concatenate([state_sconv[:, :, 1:], st["u_s"][:, :, None]], axis=2)
    cf_s = jnp.concatenate([state_cconv[:, :, 1:], st["g_s"][:, :, None]], axis=2)
    k_s = jnp.concatenate([cache_swa_k[:, :, 1:], st["k_s"].reshape(depth, n_dec, 1, N_KV_HEADS, HEAD_DIM)], axis=2)
    v_s = jnp.concatenate([cache_swa_v[:, :, 1:], st["v_s"].reshape(depth, n_dec, 1, N_KV_HEADS, HEAD_DIM)], axis=2)
    S_s = st["S_s"].reshape(depth, n_dec, GLA_HEADS, GLA_DK, GLA_DV)
    return (yp.reshape(batch, seq, D_MODEL), ys.reshape(n_dec, 1, D_MODEL),
            sc_p, sc_s, cf_p, cf_s, k_p, k_s, v_p, v_s, S_p, S_s)
```
